```python
import math, functools
import jax, jax.numpy as jnp
from jax import lax
import numpy as np

D_MODEL = 1024
BATCH = 2
SEQ = 8192
DEPTH = 2
DEC_BATCH = 32
DEC_SEQ = 4
PAST_LEN = 8192
PAGE_SIZE = 128

DH_A = 64
H_A = D_MODEL // 128
W_A = H_A * DH_A
MOBA_BLOCK = 256
MOBA_TOPK = 3
Q_BLOCK = 128
N_BUCKETS = 32
MAX_EXACT = 16
MAX_DIST = 128
W_B = D_MODEL // 2
CONV_B = 3
DK_C = 64
DV_C = 64
H_C = D_MODEL // 128
W_C = H_C * DV_C
CONV_C = 4
GDN_CHUNK = 64
W_MIX = W_A + W_B + W_C
N_IN = 4 * W_A + 4 * W_B + 4 * W_C + 2 * H_C
N_MEM = 256
H_X = 4
DH_X = D_MODEL // H_X
DN_ALPHA = (2 * DEPTH) ** 0.25
DN_BETA = (8 * DEPTH) ** -0.25
LN_EPS = 1e-5
RMS_EPS = 1e-6
NEG = -1e30

kernel_name = 'hybrid_moba_conv_gdn_decoder_step'


def layer_norm(x, g, b):
    xf = x.astype(jnp.float32)
    mu = jnp.mean(xf, -1, keepdims=True)
    var = jnp.mean(jnp.square(xf - mu), -1, keepdims=True)
    return ((xf - mu) * lax.rsqrt(var + LN_EPS) * g + b).astype(x.dtype)


def l2norm(a):
    af = a.astype(jnp.float32)
    return af * lax.rsqrt(jnp.sum(af * af, -1, keepdims=True) + RMS_EPS)


def t5_bucket(dist):
    n = jnp.maximum(dist, 0)
    large = MAX_EXACT + (jnp.log(jnp.maximum(n, 1).astype(jnp.float32) / MAX_EXACT)
                         / math.log(MAX_DIST / MAX_EXACT) * (N_BUCKETS - MAX_EXACT)).astype(jnp.int32)
    large = jnp.minimum(large, N_BUCKETS - 1)
    return jnp.where(n < MAX_EXACT, n, large)


def causal_dwconv(x, w, prev):
    k = w.shape[0]
    t = x.shape[1]
    xp = jnp.concatenate([prev.astype(x.dtype), x], axis=1)
    y = xp[:, :t] * w[0]
    for j in range(1, k):
        y = y + xp[:, j:j + t] * w[j]
    return y, xp[:, t:]


def moba_attend(q, qpos, kb, vb, kmean, n_past_blocks, k_sel, k_own, v_own, pos_own, rel_t):
    bsz, nh, tq, dh = q.shape
    qf = q.astype(jnp.float32) * dh ** -0.5
    dist_o = qpos[:, None] - pos_own[None, :]
    lo = jnp.einsum('bhqd,bhld->bhql', qf, k_own.astype(jnp.float32)) + rel_t[:, t5_bucket(dist_o)][None]
    lo = jnp.where((dist_o >= 0)[None, None], lo, NEG)
    if k_sel == 0:
        p = jax.nn.softmax(lo, axis=-1)
        return jnp.einsum('bhql,bhld->bhqd', p.astype(v_own.dtype), v_own)
    nb = kmean.shape[2]
    gate = jnp.einsum('bhqd,bhnd->bhqn', qf, kmean)
    gate = jnp.where(jnp.arange(nb) < n_past_blocks, gate, NEG)
    _, idx = lax.top_k(gate, k_sel)
    bi = jnp.arange(bsz)[:, None, None, None]
    hi = jnp.arange(nh)[None, :, None, None]
    kg = kb[bi, hi, idx]
    vg = vb[bi, hi, idx]
    pos_g = idx[..., None] * MOBA_BLOCK + jnp.arange(MOBA_BLOCK)
    dist_g = qpos[None, None, :, None, None] - pos_g
    lg = jnp.einsum('bhqd,bhqkld->bhqkl', qf, kg.astype(jnp.float32)) + rel_t[hi[..., None], t5_bucket(dist_g)]
    lg = jnp.where((idx < n_past_blocks)[..., None], lg, NEG)
    n_g = k_sel * MOBA_BLOCK
    p = jax.nn.softmax(jnp.concatenate([lg.reshape(bsz, nh, tq, n_g), lo], axis=-1), axis=-1)
    pg = p[..., :n_g].reshape(lg.shape).astype(vg.dtype)
    po = p[..., n_g:].astype(v_own.dtype)
    return jnp.einsum('bhqkl,bhqkld->bhqd', pg, vg) + jnp.einsum('bhql,bhld->bhqd', po, v_own)


def moba_prompt(q, k, v, rel_t):
    bsz, s, nh, dh = q.shape
    nb = -(-s // MOBA_BLOCK)
    pad = nb * MOBA_BLOCK - s

    def blocks(a):
        a = jnp.pad(a, ((0, 0), (0, pad), (0, 0), (0, 0)))
        return a.reshape(bsz, nb, MOBA_BLOCK, nh, dh).transpose(0, 3, 1, 2, 4)

    kb, vb = blocks(k), blocks(v)
    kmean = jnp.mean(kb.astype(jnp.float32), axis=3)
    k_sel = min(MOBA_TOPK, nb - 1)
    qt = q.transpose(0, 2, 1, 3)

    def one_block(i):
        start = i * Q_BLOCK
        qb = lax.dynamic_slice_in_dim(qt, start, Q_BLOCK, axis=2)
        qpos = start + jnp.arange(Q_BLOCK)
        own = start // MOBA_BLOCK
        k_own = lax.dynamic_index_in_dim(kb, own, axis=2, keepdims=False)
        v_own = lax.dynamic_index_in_dim(vb, own, axis=2, keepdims=False)
        pos_own = own * MOBA_BLOCK + jnp.arange(MOBA_BLOCK)
        return moba_attend(qb, qpos, kb, vb, kmean, own, k_sel, k_own, v_own, pos_own, rel_t)

    out = lax.map(one_block, jnp.arange(s // Q_BLOCK))
    return out.transpose(1, 0, 3, 2, 4).reshape(bsz, s, nh * dh)


def moba_sample(q, k_new, v_new, k_past, v_past, rel_t):
    bsz, t, nh, dh = q.shape
    past = k_past.shape[1]
    nbf = past // MOBA_BLOCK
    split = nbf * MOBA_BLOCK

    def blocks(a):
        return a[:, :split].reshape(bsz, nbf, MOBA_BLOCK, nh, dh).transpose(0, 3, 1, 2, 4)

    kb, vb = blocks(k_past), blocks(v_past)
    kmean = jnp.mean(kb.astype(jnp.float32), axis=3)
    k_own = jnp.concatenate([k_past[:, split:], k_new], axis=1).transpose(0, 2, 1, 3)
    v_own = jnp.concatenate([v_past[:, split:], v_new], axis=1).transpose(0, 2, 1, 3)
    pos_own = split + jnp.arange(past - split + t)
    qpos = past + jnp.arange(t)
    out = moba_attend(q.transpose(0, 2, 1, 3), qpos, kb, vb, kmean, nbf, min(MOBA_TOPK, nbf),
                      k_own, v_own, pos_own, rel_t)
    return out.transpose(0, 2, 1, 3).reshape(bsz, t, nh * dh)


def gated_delta_chunked(q, k, v, g, beta, s0):
    bsz, t, nh, dk = q.shape
    dv = v.shape[-1]
    c = min(GDN_CHUNK, t)
    pad = (-t) % c
    nc = (t + pad) // c

    def chunks(a):
        a = jnp.moveaxis(a.astype(jnp.float32), 2, 1)
        a = jnp.pad(a, [(0, 0), (0, 0), (0, pad)] + [(0, 0)] * (a.ndim - 3))
        return a.reshape(a.shape[:2] + (nc, c) + a.shape[3:])

    q = chunks(q) * dk ** -0.5
    k = chunks(k)
    v = chunks(v)
    g = chunks(g)
    beta = chunks(beta)
    gc = jnp.cumsum(g, axis=-1)
    kbeta = k * beta[..., None]
    vbeta = v * beta[..., None]
    tri = jnp.tril(jnp.ones((c, c), dtype=bool))
    strict = jnp.tril(jnp.ones((c, c), dtype=bool), -1)
    diff = gc[..., :, None] - gc[..., None, :]
    decay = jnp.where(tri, jnp.exp(jnp.where(tri, diff, 0.0)), 0.0)
    a_mat = jnp.where(strict, jnp.einsum('bhncd,bhnsd->bhncs', kbeta, k) * decay, 0.0)
    eye = jnp.eye(c, dtype=jnp.float32)
    tinv = lax.linalg.triangular_solve(a_mat + eye, jnp.broadcast_to(eye, a_mat.shape),
                                       left_side=True, lower=True, unit_diagonal=True)
    u = jnp.einsum('bhncs,bhnsv->bhncv', tinv, vbeta)
    w = jnp.einsum('bhncs,bhnsd->bhncd', tinv, kbeta * jnp.exp(gc)[..., None])
    xs = tuple(jnp.moveaxis(a, 2, 0) for a in (q, k, u, w, gc, decay))

    def step(s, inp):
        qi, ki, ui, wi, gi, di = inp
        v_new = ui - jnp.einsum('bhcd,bhdv->bhcv', wi, s)
        attn = jnp.einsum('bhcd,bhsd->bhcs', qi, ki) * di
        o = (jnp.einsum('bhcd,bhdv->bhcv', qi * jnp.exp(gi)[..., None], s)
             + jnp.einsum('bhcs,bhsv->bhcv', attn, v_new))
        g_last = gi[..., -1:]
        s = (s * jnp.exp(g_last)[..., None]
             + jnp.einsum('bhcd,bhcv->bhdv', ki * jnp.exp(g_last - gi)[..., None], v_new))
        return s, o

    s_fin, o = lax.scan(step, s0.astype(jnp.float32), xs)
    o = jnp.moveaxis(o, 0, 2).reshape(bsz, nh, nc * c, dv)[:, :, :t]
    return jnp.moveaxis(o, 1, 2), s_fin


def gdn_branch(qkv, a, b, conv_w, a_log, dt_bias, norm_w, conv_prev, s0):
    bsz, t, _ = qkv.shape
    qkv, conv_new = causal_dwconv(qkv, conv_w, conv_prev)
    qkv = jax.nn.silu(qkv)
    q, k, v = jnp.split(qkv, 3, axis=-1)
    q = l2norm(q.reshape(bsz, t, H_C, DK_C))
    k = l2norm(k.reshape(bsz, t, H_C, DK_C))
    v = v.reshape(bsz, t, H_C, DV_C)
    g = -jnp.exp(a_log.astype(jnp.float32)) * jax.nn.softplus(a.astype(jnp.float32) + dt_bias)
    beta = jax.nn.sigmoid(b.astype(jnp.float32))
    o, s_new = gated_delta_chunked(q, k, v, g, beta, s0)
    o = o * lax.rsqrt(jnp.mean(o * o, -1, keepdims=True) + RMS_EPS) * norm_w
    return o.reshape(bsz, t, W_C).astype(qkv.dtype), conv_new, s_new


def split_in(z):
    sizes = [W_A, W_A, W_A, W_A, W_B, W_B, W_B, W_B, 3 * W_C, W_C, H_C]
    return jnp.split(z, [int(i) for i in np.cumsum(sizes)], axis=-1)


def mixer(x, attn_fn, conv_b_prev, conv_c_prev, s0, w_in, conv_b_w, conv_c_w, a_log, dt_bias, norm_w, w_out):
    bsz, t, _ = x.shape
    z = x @ w_in
    qa, ka, va, ga, bb, cb, hb, gb, qkvc, gcz, ac, bc = split_in(z)
    qa, ka, va = [a.reshape(bsz, t, H_A, DH_A) for a in (qa, ka, va)]
    ya = jax.nn.silu(ga) * attn_fn(qa, ka, va)
    conv_out, conv_b_new = causal_dwconv(cb * hb, conv_b_w, conv_b_prev)
    yb = jax.nn.silu(gb) * (bb * conv_out)
    yc, conv_c_new, s_new = gdn_branch(qkvc, ac, bc, conv_c_w, a_log, dt_bias, norm_w, conv_c_prev, s0)
    yc = jax.nn.silu(gcz) * yc
    y = jnp.concatenate([ya, yb, yc], axis=-1) @ w_out
    return y, ka, va, conv_b_new, conv_c_new, s_new


def memory_kv(mem, wk, wv):
    b, m, _ = mem.shape
    return (mem @ wk).reshape(b, m, H_X, DH_X), (mem @ wv).reshape(b, m, H_X, DH_X)


def cross_attn(x, mk, mv, wq, wo):
    b, t, _ = x.shape
    q = (x @ wq).reshape(b, t, H_X, DH_X).astype(jnp.float32) * DH_X ** -0.5
    p = jax.nn.softmax(jnp.einsum('bthd,bmhd->bhtm', q, mk.astype(jnp.float32)), axis=-1)
    o = jnp.einsum('bhtm,bmhd->bthd', p.astype(mv.dtype), mv).reshape(b, t, H_X * DH_X)
    return o @ wo


def setup_inputs(seed: int = 0) -> dict:
    key = jax.random.key(seed)
    ks = jax.random.split(key, 32)
    f32 = jnp.float32

    def nrm(k, shape, s=1.0):
        return jax.random.normal(k, shape, f32) * s

    n_pages = PAST_LEN // PAGE_SIZE
    used = DEC_BATCH * n_pages
    n_phys = used + max(1, used // 4)
    page_table = jax.random.permutation(ks[0], n_phys)[:used].reshape(DEC_BATCH, n_pages).astype(jnp.int32)
    dt = jnp.exp(jax.random.uniform(ks[1], (DEPTH, H_C), f32, math.log(1e-3), math.log(1e-1)))
    dt_bias = dt + jnp.log(-jnp.expm1(-dt))
    a_log = jnp.log(jax.random.uniform(ks[2], (DEPTH, H_C), f32, 1.0, 16.0))
    return {
        'x_prompt': nrm(ks[3], (BATCH, SEQ, D_MODEL)),
        'x_sample': nrm(ks[4], (DEC_BATCH, DEC_SEQ, D_MODEL)),
        'mem_prompt': nrm(ks[5], (BATCH, N_MEM, D_MODEL)),
        'cache_attn_k': nrm(ks[6], (DEPTH, n_phys, PAGE_SIZE, H_A, DH_A)),
        'cache_attn_v': nrm(ks[7], (DEPTH, n_phys, PAGE_SIZE, H_A, DH_A)),
        'cache_mem_k': nrm(ks[8], (DEPTH, DEC_BATCH, N_MEM, H_X, DH_X)),
        'cache_mem_v': nrm(ks[9], (DEPTH, DEC_BATCH, N_MEM, H_X, DH_X)),
        'state_conv_b': nrm(ks[10], (DEPTH, DEC_BATCH, CONV_B - 1, W_B)),
        'state_conv_c': nrm(ks[11], (DEPTH, DEC_BATCH, CONV_C - 1, 3 * W_C)),
        'state_delta': nrm(ks[12], (DEPTH, DEC_BATCH, H_C, DK_C, DV_C), 0.5),
        'page_table': page_table,
        'w_in': nrm(ks[13], (DEPTH, D_MODEL, N_IN), D_MODEL ** -0.5),
        'conv_b_w': nrm(ks[14], (DEPTH, CONV_B, W_B), CONV_B ** -0.5),
        'conv_c_w': nrm(ks[15], (DEPTH, CONV_C, 3 * W_C), CONV_C ** -0.5),
        'gdn_a_log': a_log,
        'gdn_dt_bias': dt_bias,
        'gdn_norm_w': 1.0 + nrm(ks[16], (DEPTH, DV_C), 0.02),
        'w_out': nrm(ks[17], (DEPTH, W_MIX, D_MODEL), W_MIX ** -0.5 * DN_BETA),
        'ln1_g': 1.0 + nrm(ks[18], (DEPTH, D_MODEL), 0.02),
        'ln1_b': nrm(ks[19], (DEPTH, D_MODEL), 0.02),
        'rel_bias': nrm(ks[20], (N_BUCKETS, H_A), 0.2),
        'wq_x': nrm(ks[21], (DEPTH, D_MODEL, H_X * DH_X), D_MODEL ** -0.5),
        'wk_x': nrm(ks[22], (DEPTH, D_MODEL, H_X * DH_X), D_MODEL ** -0.5),
        'wv_x': nrm(ks[23], (DEPTH, D_MODEL, H_X * DH_X), D_MODEL ** -0.5),
        'wo_x': nrm(ks[24], (DEPTH, H_X * DH_X, D_MODEL), D_MODEL ** -0.5 * DN_BETA),
        'ln2_g': 1.0 + nrm(ks[25], (DEPTH, D_MODEL), 0.02),
        'ln2_b': nrm(ks[26], (DEPTH, D_MODEL), 0.02),
    }


def reference(x_prompt, x_sample, mem_prompt, cache_attn_k, cache_attn_v, cache_mem_k, cache_mem_v,
              state_conv_b, state_conv_c, state_delta, page_table,
              w_in, conv_b_w, conv_c_w, gdn_a_log, gdn_dt_bias, gdn_norm_w, w_out, ln1_g, ln1_b,
              rel_bias, wq_x, wk_x, wv_x, wo_x, ln2_g, ln2_b):
    rel_t = rel_bias.T
    xp, xs = x_prompt, x_sample
    bp, bs = xp.shape[0], xs.shape[0]
    kp, vp, ksm, vsm, mkp, mvp, cbp, cbs, ccp, ccs, sdp, sds = ([] for _ in range(12))
    for l in range(DEPTH):
        lw = (w_in[l], conv_b_w[l], conv_c_w[l], gdn_a_log[l], gdn_dt_bias[l], gdn_norm_w[l], w_out[l])
        attn_p = functools.partial(moba_prompt, rel_t=rel_t)
        y, k, v, cb, cc, sd = mixer(xp, attn_p,
                                    jnp.zeros((bp, CONV_B - 1, W_B), xp.dtype),
                                    jnp.zeros((bp, CONV_C - 1, 3 * W_C), xp.dtype),
                                    jnp.zeros((bp, H_C, DK_C, DV_C), jnp.float32), *lw)
        xp = layer_norm(DN_ALPHA * xp + y, ln1_g[l], ln1_b[l])
        mk, mv = memory_kv(mem_prompt, wk_x[l], wv_x[l])
        xp = layer_norm(DN_ALPHA * xp + cross_attn(xp, mk, mv, wq_x[l], wo_x[l]), ln2_g[l], ln2_b[l])
        kp.append(k); vp.append(v); mkp.append(mk); mvp.append(mv)
        cbp.append(cb); ccp.append(cc); sdp.append(sd)
        k_past = cache_attn_k[l][page_table].reshape(bs, -1, H_A, DH_A)
        v_past = cache_attn_v[l][page_table].reshape(bs, -1, H_A, DH_A)
        attn_s = functools.partial(moba_sample, k_past=k_past, v_past=v_past, rel_t=rel_t)
        y, k, v, cb, cc, sd = mixer(xs, attn_s, state_conv_b[l], state_conv_c[l], state_delta[l], *lw)
        xs = layer_norm(DN_ALPHA * xs + y, ln1_g[l], ln1_b[l])
        xs = layer_norm(DN_ALPHA * xs + cross_attn(xs, cache_mem_k[l], cache_mem_v[l], wq_x[l], wo_x[l]),
                        ln2_g[l], ln2_b[l])
        ksm.append(k); vsm.append(v); cbs.append(cb); ccs.append(cc); sds.append(sd)
    return (xp, xs,
            jnp.stack(kp), jnp.stack(vp), jnp.stack(ksm), jnp.stack(vsm),
            jnp.stack(mkp), jnp.stack(mvp),
            jnp.stack(cbp), jnp.stack(cbs), jnp.stack(ccp), jnp.stack(ccs),
            jnp.stack(sdp), jnp.stack(sds))
```

```python
import functools
import math

import numpy as np
import jax
import jax.numpy as jnp
from jax import lax
from jax.experimental import pallas as pl
from jax.experimental.pallas import tpu as pltpu

F32 = jnp.float32
BF16 = jnp.bfloat16

LANE = 128
SUBLANE = 8
VMEM_LIMIT = 56 * 1024 * 1024

DH_A = 64
MOBA_BLOCK = 256
MOBA_TOPK = 3
N_BUCKETS = 32
MAX_EXACT = 16
MAX_DIST = 128
CONV_B = 3
CONV_C = 4
DK_C = 64
DV_C = 64
GDN_CHUNK = 64
H_X = 4
LN_EPS = 1e-5
RMS_EPS = 1e-6
NEG = -1e30
M_INIT = -3.0e38

NN = ((1,), (0,))
NT = ((1,), (1,))
TN = ((0,), (0,))


def _dot(a, b, dims=NN, passes=1):
    dn = (dims, ((), ()))
    if passes == 6:
        return lax.dot_general(a, b, dn, precision=lax.Precision.HIGHEST, preferred_element_type=F32)
    ah = a.astype(BF16)
    bh = b.astype(BF16)
    out = lax.dot_general(ah, bh, dn, preferred_element_type=F32)
    if passes == 3:
        al = (a - ah.astype(F32)).astype(BF16)
        bl = (b - bh.astype(F32)).astype(BF16)
        out = out + lax.dot_general(ah, bl, dn, preferred_element_type=F32)
        out = out + lax.dot_general(al, bh, dn, preferred_element_type=F32)
    return out


def _silu(x):
    return x * (1.0 / (1.0 + jnp.exp(-x)))


def _params(*sem):
    return pltpu.CompilerParams(dimension_semantics=sem, vmem_limit_bytes=VMEM_LIMIT)


def _bucket_thresholds():
    n = np.arange(0, 4 * MAX_DIST, dtype=np.int32)
    ratio = np.log(np.maximum(n, 1).astype(np.float32) / np.float32(MAX_EXACT)) / np.float32(math.log(MAX_DIST / MAX_EXACT))
    large = MAX_EXACT + (ratio * np.float32(N_BUCKETS - MAX_EXACT)).astype(np.int32)
    bucket = np.where(n < MAX_EXACT, n, np.minimum(large, N_BUCKETS - 1))
    return [int(np.argmax(bucket >= b)) for b in range(N_BUCKETS)]


_THR = _bucket_thresholds()


def _bias_kernel(rel_ref, o_ref, *, bases, rows):
    h = pl.program_id(0)
    cols = o_ref.shape[2]
    for i, base in enumerate(bases):
        dist = (base + lax.broadcasted_iota(jnp.int32, (rows, cols), 0)
                - lax.broadcasted_iota(jnp.int32, (rows, cols), 1))
        val = jnp.full((rows, cols), rel_ref[0, h], F32)
        for b in range(1, N_BUCKETS):
            val = jnp.where(dist >= _THR[b], rel_ref[b, h], val)
        o_ref[0, i * rows:(i + 1) * rows, :] = jnp.where(dist >= 0, val, NEG)


def bias_table(rel_bias, bases, rows, cols):
    nh = rel_bias.shape[1]
    return pl.pallas_call(
        functools.partial(_bias_kernel, bases=tuple(bases), rows=rows),
        grid=(nh,),
        in_specs=[pl.BlockSpec(memory_space=pltpu.SMEM)],
        out_specs=pl.BlockSpec((1, len(bases) * rows, cols), lambda h: (h, 0, 0)),
        out_shape=jax.ShapeDtypeStruct((nh, len(bases) * rows, cols), F32),
        compiler_params=_params("arbitrary"),
        name="bias_table",
    )(rel_bias)


def _mm_kernel(x_ref, w_ref, o_ref):
    o_ref[...] = jnp.dot(x_ref[...].astype(BF16), w_ref[...], preferred_element_type=F32)


def matmul(x, w, tm, tn):
    m, k = x.shape
    n = w.shape[1]
    return pl.pallas_call(
        _mm_kernel,
        grid=(n // tn, m // tm),
        in_specs=[pl.BlockSpec((tm, k), lambda j, i: (i, 0)),
                  pl.BlockSpec((k, tn), lambda j, i: (0, j))],
        out_specs=pl.BlockSpec((tm, tn), lambda j, i: (i, j)),
        out_shape=jax.ShapeDtypeStruct((m, n), F32),
        compiler_params=_params("arbitrary", "arbitrary"),
        name="matmul",
    )(x, w)


def _outproj_ln_kernel(*refs, n, alpha):
    a_refs, w_refs = refs[:n], refs[n:2 * n]
    x_ref, g_ref, b_ref, o_ref = refs[2 * n:]
    y = _dot(a_refs[0][...], w_refs[0][...])
    for a_ref, w_ref in zip(a_refs[1:], w_refs[1:]):
        y = y + _dot(a_ref[...], w_ref[...])
    hid = alpha * x_ref[...] + y
    mu = jnp.mean(hid, axis=-1, keepdims=True)
    cen = hid - mu
    var = jnp.mean(cen * cen, axis=-1, keepdims=True)
    o_ref[...] = cen * lax.rsqrt(var + LN_EPS) * g_ref[...] + b_ref[...]


def outproj_ln(acts, weights, x, g, b, alpha, tm):
    m, d = x.shape
    n = len(acts)
    in_specs = ([pl.BlockSpec((tm, a.shape[1]), lambda i: (i, 0)) for a in acts]
                + [pl.BlockSpec(w.shape, lambda i: (0, 0)) for w in weights]
                + [pl.BlockSpec((tm, d), lambda i: (i, 0)),
                   pl.BlockSpec((1, d), lambda i: (0, 0)),
                   pl.BlockSpec((1, d), lambda i: (0, 0))])
    return pl.pallas_call(
        functools.partial(_outproj_ln_kernel, n=n, alpha=alpha),
        grid=(m // tm,),
        in_specs=in_specs,
        out_specs=pl.BlockSpec((tm, d), lambda i: (i, 0)),
        out_shape=jax.ShapeDtypeStruct((m, d), F32),
        compiler_params=_params("arbitrary"),
        name="outproj_ln",
    )(*acts, *weights, x, g.reshape(1, d), b.reshape(1, d))


def _select_topk(gate, valid, lane, k):
    big = 1 << 20
    g = jnp.where(valid, gate, NEG)
    live_lane = jnp.where(valid, lane, big)
    sel = jnp.zeros(gate.shape, jnp.int32)
    for _ in range(k):
        m = jnp.max(g, axis=-1, keepdims=True)
        idx = jnp.min(jnp.where(g == m, live_lane, big), axis=-1, keepdims=True)
        hit = lane == idx
        sel = jnp.where(hit, 1, sel)
        live_lane = jnp.where(hit, big, live_lane)
        g = jnp.where(hit, NEG, g)
    return sel


def _moba_prompt_kernel(q_ref, k_ref, v_ref, g_ref, bias_ref, o_ref, kaug, vaug, kmat, *, nb):
    qi = pl.program_id(2)
    blk = MOBA_BLOCK
    lane = lax.broadcasted_iota(jnp.int32, (blk, LANE), 1)

    @pl.when(qi == 0)
    def _():
        s = k_ref.shape[1]
        kmean = jnp.sum(k_ref[0].reshape(nb, blk, LANE), axis=1) * (1.0 / blk)
        lane_nb = lax.broadcasted_iota(jnp.int32, (nb, LANE), 1)
        for hh in range(2):
            free0 = (1 - hh) * DH_A
            kmat[hh] = jnp.zeros((LANE, LANE), F32)
            kmat[hh, free0:free0 + nb, :] = jnp.where(lane_nb // DH_A == hh, kmean, 0.0)

        def fill(j, carry):
            rows = pl.ds(pl.multiple_of(j * blk, blk), blk)
            kj = k_ref[0, rows, :]
            vj = v_ref[0, rows, :]
            for hh in range(2):
                free0 = (1 - hh) * DH_A
                own = lane // DH_A == hh
                kaug[hh, rows, :] = jnp.where(own, kj, jnp.where(lane == free0 + j, 1.0, 0.0)).astype(BF16)
                vaug[hh, rows, :] = jnp.where(own, vj, 1.0).astype(BF16)
            return carry

        lax.fori_loop(0, s // blk, fill, 0)

    q = q_ref[0] * (DH_A ** -0.5)
    outs = []
    for hh in range(2):
        free0 = (1 - hh) * DH_A
        own = lane // DH_A == hh
        qh = jnp.where(own, q, 0.0)
        gate = _dot(qh, kmat[hh], NT, passes=6)
        blk_lane = lane - free0
        sel = _select_topk(gate, (blk_lane >= 0) & (blk_lane < qi), lane, MOBA_TOPK)
        attend = jnp.where(blk_lane == qi, 1, sel)
        masked = jnp.where((blk_lane >= 0) & (blk_lane < nb), 1 - attend, 0)
        q_aug = jnp.where(own, q, jnp.where(masked > 0, NEG, 0.0)).astype(BF16)

        def step(jj, carry, hh=hh, q_aug=q_aug):
            m, acc = carry
            j = qi - jj
            rows = pl.ds(pl.multiple_of(j * blk, blk), blk)
            s = _dot(q_aug, kaug[hh, rows, :], NT) + bias_ref[hh, jnp.minimum(jj, 2)]
            m_new = jnp.maximum(m, jnp.max(s, axis=-1, keepdims=True))
            p = jnp.exp(s - m_new)
            acc = jnp.exp(m - m_new) * acc + _dot(p, vaug[hh, rows, :])
            return m_new, acc

        _, acc = lax.fori_loop(0, qi + 1, step,
                               (jnp.full((blk, 1), M_INIT, F32), jnp.zeros((blk, LANE), F32)))
        outs.append(acc / pltpu.roll(acc, DH_A, axis=1))
    o_ref[0] = _silu(g_ref[0]) * jnp.where(lane < DH_A, outs[0], outs[1])


def moba_prompt(z, bias):
    bsz, s, _ = z.shape
    nh = bias.shape[0]
    npair = nh // 2
    nb = s // MOBA_BLOCK
    assert s % MOBA_BLOCK == 0 and nb <= DH_A and 2 * DH_A == LANE
    blk = MOBA_BLOCK
    return pl.pallas_call(
        functools.partial(_moba_prompt_kernel, nb=nb),
        grid=(bsz, npair, nb),
        in_specs=[pl.BlockSpec((1, blk, LANE), lambda b, p, i: (b, i, p)),
                  pl.BlockSpec((1, s, LANE), lambda b, p, i: (b, 0, npair + p)),
                  pl.BlockSpec((1, s, LANE), lambda b, p, i: (b, 0, 2 * npair + p)),
                  pl.BlockSpec((1, blk, LANE), lambda b, p, i: (b, i, 3 * npair + p)),
                  pl.BlockSpec((2, 3, blk, blk), lambda b, p, i: (p, 0, 0, 0))],
        out_specs=pl.BlockSpec((1, blk, LANE), lambda b, p, i: (b, i, p)),
        out_shape=jax.ShapeDtypeStruct((bsz, s, nh * DH_A), F32),
        scratch_shapes=[pltpu.VMEM((2, s, LANE), BF16), pltpu.VMEM((2, s, LANE), BF16),
                        pltpu.VMEM((2, LANE, LANE), F32)],
        compiler_params=_params("arbitrary", "arbitrary", "arbitrary"),
        name="moba_prompt",
    )(z, z, z, z, bias)


def _moba_sample_kernel(pt_ref, q_ref, kn_ref, vn_ref, g_ref, ka_ref, kb_ref, va_ref, vb_ref,
                        bias_past_ref, bias_own_ref, o_ref, qexp, acc_scr, m_all, l_all, gate_all, *, nb, nh):
    del pt_ref
    j = pl.program_id(1)
    t8 = SUBLANE
    rows = nh * t8
    width = nh * DH_A
    row_head = lax.broadcasted_iota(jnp.int32, (rows, width), 0) // t8
    lane_head = lax.broadcasted_iota(jnp.int32, (rows, width), 1) // DH_A
    head_mask = row_head == lane_head
    lane = lax.broadcasted_iota(jnp.int32, (rows, LANE), 1)

    @pl.when(j == 0)
    def _():
        q8 = q_ref[0] * (DH_A ** -0.5)
        qexp[...] = jnp.where(head_mask, jnp.concatenate([q8] * nh, axis=0), 0.0)
        m_all[...] = jnp.zeros((rows, LANE), F32)
        l_all[...] = jnp.zeros((rows, LANE), F32)
        gate_all[...] = jnp.zeros((rows, LANE), F32)

    qe = qexp[...]
    kblk = jnp.concatenate([ka_ref[0], kb_ref[0]], axis=0)
    vblk = jnp.concatenate([va_ref[0], vb_ref[0]], axis=0)
    kmean = jnp.sum(kblk, axis=0, keepdims=True) * (1.0 / MOBA_BLOCK)
    gate = jnp.sum(qe * kmean, axis=-1, keepdims=True)
    s = _dot(qe, kblk, NT) + bias_past_ref[jnp.where(j == nb - 1, 0, 1)]
    m_j = jnp.max(s, axis=-1, keepdims=True)
    p = jnp.exp(s - m_j)
    l_j = jnp.sum(p, axis=-1, keepdims=True)
    acc_scr[j] = _dot(p, vblk)
    m_all[...] = jnp.where(lane == j, m_j, m_all[...])
    l_all[...] = jnp.where(lane == j, l_j, l_all[...])
    gate_all[...] = jnp.where(lane == j, gate, gate_all[...])

    @pl.when(j == nb - 1)
    def _():
        pad = jnp.zeros((LANE - t8, width), F32)
        s_o = _dot(qe, jnp.concatenate([kn_ref[0], pad], axis=0), NT) + bias_own_ref[...]
        m_o = jnp.max(s_o, axis=-1, keepdims=True)
        p_o = jnp.exp(s_o - m_o)
        l_o = jnp.sum(p_o, axis=-1, keepdims=True)
        acc_o = _dot(p_o, jnp.concatenate([vn_ref[0], pad], axis=0))
        sel = _select_topk(gate_all[...], lane < nb, lane, min(MOBA_TOPK, nb)) > 0
        mm = m_all[...]
        m_tot = jnp.maximum(m_o, jnp.max(jnp.where(sel, mm, M_INIT), axis=-1, keepdims=True))
        w_all = jnp.where(sel, jnp.exp(jnp.where(sel, mm, m_tot) - m_tot), 0.0)
        w_o = jnp.exp(m_o - m_tot)
        l_tot = w_o * l_o + jnp.sum(w_all * l_all[...], axis=-1, keepdims=True)
        tot = w_o * acc_o
        for jb in range(nb):
            tot = tot + w_all[:, jb:jb + 1] * acc_scr[jb]
        tot = jnp.where(head_mask, tot / l_tot, 0.0)
        out8 = tot[0:t8]
        for h in range(1, nh):
            out8 = out8 + tot[h * t8:(h + 1) * t8]
        o_ref[0] = _silu(g_ref[0]) * out8


def moba_sample(z, cache_k, cache_v, page_table, bias_past, bias_own):
    bs, t8, _ = z.shape
    page = cache_k.shape[1]
    width = cache_k.shape[2]
    nh = width // DH_A
    n_pages = page_table.shape[1]
    assert t8 == SUBLANE and MOBA_BLOCK == 2 * page and (n_pages * page) % MOBA_BLOCK == 0
    nb = n_pages * page // MOBA_BLOCK
    assert nb <= LANE
    rows = nh * t8
    zspec = lambda c: pl.BlockSpec((1, t8, width), lambda b, j, pt: (b, 0, c))
    pspec = lambda o: pl.BlockSpec((1, page, width), lambda b, j, pt: (pt[b, 2 * j + o], 0, 0))
    grid_spec = pltpu.PrefetchScalarGridSpec(
        num_scalar_prefetch=1,
        grid=(bs, nb),
        in_specs=[zspec(0), zspec(1), zspec(2), zspec(3), pspec(0), pspec(1), pspec(0), pspec(1),
                  pl.BlockSpec((2, rows, MOBA_BLOCK), lambda b, j, pt: (0, 0, 0)),
                  pl.BlockSpec((rows, LANE), lambda b, j, pt: (0, 0))],
        out_specs=pl.BlockSpec((1, t8, width), lambda b, j, pt: (b, 0, 0)),
        scratch_shapes=[pltpu.VMEM((rows, width), F32), pltpu.VMEM((nb, rows, width), F32),
                        pltpu.VMEM((rows, LANE), F32), pltpu.VMEM((rows, LANE), F32),
                        pltpu.VMEM((rows, LANE), F32)])
    return pl.pallas_call(
        functools.partial(_moba_sample_kernel, nb=nb, nh=nh),
        grid_spec=grid_spec,
        out_shape=jax.ShapeDtypeStruct((bs, t8, width), F32),
        compiler_params=_params("arbitrary", "arbitrary"),
        name="moba_sample",
    )(page_table, z, z, z, z, cache_k, cache_k, cache_v, cache_v, bias_past, bias_own)


def _conv_b_kernel(b_ref, c_ref, h_ref, g_ref, prev_ref, w_ref, y_ref, tail_ref, buf):
    r = c_ref.shape[1]

    @pl.when(pl.program_id(1) == 0)
    def _():
        buf[0:SUBLANE, :] = prev_ref[0]

    u = c_ref[0] * h_ref[0]
    buf[SUBLANE:SUBLANE + r, :] = u
    conv = u * w_ref[CONV_B - 1:CONV_B, :]
    for jtap in range(CONV_B - 1):
        off = SUBLANE - (CONV_B - 1) + jtap
        conv = conv + buf[off:off + r, :] * w_ref[jtap:jtap + 1, :]
    y_ref[0] = _silu(g_ref[0]) * (b_ref[0] * conv)
    tail_ref[0] = u[r - SUBLANE:, :]
    buf[0:SUBLANE, :] = u[r - SUBLANE:, :]


def conv_b(z, prev8, w8, tr, col0):
    bsz, t, _ = z.shape
    width = prev8.shape[2]
    zspec = lambda c: pl.BlockSpec((1, tr, width), lambda b, i: (b, i, c))
    return pl.pallas_call(
        _conv_b_kernel,
        grid=(bsz, t // tr),
        in_specs=[zspec(col0), zspec(col0 + 1), zspec(col0 + 2), zspec(col0 + 3),
                  pl.BlockSpec((1, SUBLANE, width), lambda b, i: (b, 0, 0)),
                  pl.BlockSpec((SUBLANE, width), lambda b, i: (0, 0))],
        out_specs=[pl.BlockSpec((1, tr, width), lambda b, i: (b, i, 0)),
                   pl.BlockSpec((1, SUBLANE, width), lambda b, i: (b, 0, 0))],
        out_shape=[jax.ShapeDtypeStruct((bsz, t, width), F32),
                   jax.ShapeDtypeStruct((bsz, SUBLANE, width), F32)],
        scratch_shapes=[pltpu.VMEM((tr + SUBLANE, width), F32)],
        compiler_params=_params("arbitrary", "arbitrary"),
        name="conv_b",
    )(z, z, z, z, prev8, w8)


def _gdn_pre_kernel(xq_ref, xk_ref, xv_ref, pq_ref, pk_ref, pv_ref, w_ref, q_ref, k_ref, v_ref, buf, *, nh):
    r = xq_ref.shape[1]
    width = xq_ref.shape[2]
    first = pl.program_id(1) == 0
    for idx, (x_ref, p_ref, o_ref) in enumerate(((xq_ref, pq_ref, q_ref), (xk_ref, pk_ref, k_ref),
                                                 (xv_ref, pv_ref, v_ref))):
        @pl.when(first)
        def _(idx=idx, p_ref=p_ref):
            buf[idx, 0:SUBLANE, :] = p_ref[0]

        x = x_ref[0]
        buf[idx, SUBLANE:SUBLANE + r, :] = x
        wcol = slice(idx * width, (idx + 1) * width)
        conv = x * w_ref[CONV_C - 1:CONV_C, wcol]
        for jtap in range(CONV_C - 1):
            off = SUBLANE - (CONV_C - 1) + jtap
            conv = conv + buf[idx, off:off + r, :] * w_ref[jtap:jtap + 1, wcol]
        buf[idx, 0:SUBLANE, :] = x[r - SUBLANE:, :]
        act = _silu(conv)
        for h in range(nh):
            a = act[:, h * DK_C:(h + 1) * DK_C]
            if idx < 2:
                a = a * lax.rsqrt(jnp.sum(a * a, axis=-1, keepdims=True) + RMS_EPS)
            o_ref[0, h] = a


def gdn_pre(z, prev8, w8, tr, col0):
    bsz, t, _ = z.shape
    width = prev8.shape[2] // 3
    nh = width // DK_C
    zspec = lambda c: pl.BlockSpec((1, tr, width), lambda b, i: (b, i, c))
    pspec = lambda c: pl.BlockSpec((1, SUBLANE, width), lambda b, i: (b, 0, c))
    ospec = pl.BlockSpec((1, nh, tr, DK_C), lambda b, i: (b, 0, i, 0))
    oshape = jax.ShapeDtypeStruct((bsz, nh, t, DK_C), F32)
    return pl.pallas_call(
        functools.partial(_gdn_pre_kernel, nh=nh),
        grid=(bsz, t // tr),
        in_specs=[zspec(col0), zspec(col0 + 1), zspec(col0 + 2), pspec(0), pspec(1), pspec(2),
                  pl.BlockSpec((SUBLANE, 3 * width), lambda b, i: (0, 0))],
        out_specs=[ospec, ospec, ospec],
        out_shape=[oshape, oshape, oshape],
        scratch_shapes=[pltpu.VMEM((3, tr + SUBLANE, width), F32)],
        compiler_params=_params("arbitrary", "arbitrary"),
        name="gdn_pre",
    )(z, z, z, prev8, prev8, prev8, w8)


def _gdn_chunk_kernel(q_ref, k_ref, v_ref, ab_ref, alog_ref, dtb_ref, s0_ref, o_ref, sout_ref, s_scr,
                      *, nh, t_valid, passes):
    c = GDN_CHUNK
    ti = pl.program_id(1)
    tc = q_ref.shape[2]

    @pl.when(ti == 0)
    def _():
        s_scr[...] = s0_ref[0]

    row = lax.broadcasted_iota(jnp.int32, (c, c), 0)
    col = lax.broadcasted_iota(jnp.int32, (c, c), 1)
    tri = row >= col
    strict = row > col
    eye = jnp.where(row == col, 1.0, 0.0)
    tril_ones = jnp.where(tri, 1.0, 0.0)
    lane_c = lax.broadcasted_iota(jnp.int32, (c, LANE), 1)
    neg_a = -jnp.exp(alog_ref[...])
    dtb = dtb_ref[...]

    def chunk(ci, carry):
        r0 = pl.multiple_of(ci * c, c)
        ab = ab_ref[0, pl.ds(r0, c), :]
        g_all = neg_a * (jnp.maximum(ab + dtb, 0.0) + jnp.log(1.0 + jnp.exp(-jnp.abs(ab + dtb))))
        beta_all = 1.0 / (1.0 + jnp.exp(-ab))
        if t_valid is not None:
            live = (ti * tc + r0 + lax.broadcasted_iota(jnp.int32, (c, LANE), 0)) < t_valid
            g_all = jnp.where(live, g_all, 0.0)
            beta_all = jnp.where(live, beta_all, 0.0)
        gc_all = _dot(tril_ones, g_all, NN, passes=6)
        for h in range(nh):
            gcol = gc_all[:, h:h + 1]
            bcol = beta_all[:, nh + h:nh + h + 1]
            grow = _dot(jnp.where(lane_c == h, 1.0, 0.0), gc_all, NT, passes=6)
            decay = jnp.where(tri, jnp.exp(jnp.where(tri, gcol - grow, 0.0)), 0.0)
            qh = q_ref[0, h, pl.ds(r0, c), :] * (DK_C ** -0.5)
            kh = k_ref[0, h, pl.ds(r0, c), :]
            vh = v_ref[0, h, pl.ds(r0, c), :]
            kbeta = kh * bcol
            a_mat = jnp.where(strict, _dot(kbeta, kh, NT, passes) * decay, 0.0)
            tinv = eye - a_mat
            power = a_mat
            for _ in range(int(math.log2(c)) - 1):
                power = _dot(power, power, NN, passes)
                tinv = tinv + _dot(tinv, power, NN, passes)
            egc = jnp.exp(gcol)
            u = _dot(tinv, vh * bcol, NN, passes)
            w = _dot(tinv, kbeta * egc, NN, passes)
            s_h = s_scr[h]
            v_new = u - _dot(w, s_h, NN, passes)
            attn = _dot(qh, kh, NT, passes) * decay
            o_ref[0, h, pl.ds(r0, c), :] = _dot(qh * egc, s_h, NN, passes) + _dot(attn, v_new, NN, passes)
            g_last = gc_all[c - 1:c, h:h + 1]
            s_scr[h] = s_h * jnp.exp(g_last) + _dot(kh * jnp.exp(g_last - gcol), v_new, TN, passes)
        return carry

    lax.fori_loop(0, tc // c, chunk, 0)

    @pl.when(ti == pl.num_programs(1) - 1)
    def _():
        sout_ref[0] = s_scr[...]


def gdn_chunk(q, k, v, ab, ab_col, alog_lane, dtb_lane, s0, tc, t_valid, passes):
    bsz, nh, t, _ = q.shape
    assert t % tc == 0 and tc % GDN_CHUNK == 0
    qspec = pl.BlockSpec((1, nh, tc, DK_C), lambda b, i: (b, 0, i, 0))
    sspec = pl.BlockSpec((1, nh, DK_C, DV_C), lambda b, i: (b, 0, 0, 0))
    vec = pl.BlockSpec((1, LANE), lambda b, i: (0, 0))
    return pl.pallas_call(
        functools.partial(_gdn_chunk_kernel, nh=nh, t_valid=None if t_valid == t else t_valid, passes=passes),
        grid=(bsz, t // tc),
        in_specs=[qspec, qspec, qspec, pl.BlockSpec((1, tc, LANE), lambda b, i: (b, i, ab_col)), vec, vec, sspec],
        out_specs=[qspec, sspec],
        out_shape=[jax.ShapeDtypeStruct((bsz, nh, t, DV_C), F32),
                   jax.ShapeDtypeStruct((bsz, nh, DK_C, DV_C), F32)],
        scratch_shapes=[pltpu.VMEM((nh, DK_C, DV_C), F32)],
        compiler_params=_params("arbitrary", "arbitrary"),
        name="gdn_chunk",
    )(q, k, v, ab, alog_lane, dtb_lane, s0)


def _gdn_post_kernel(o_ref, nw_ref, g_ref, y_ref, *, nh):
    parts = []
    for h in range(nh):
        o = o_ref[0, h]
        parts.append(o * lax.rsqrt(jnp.mean(o * o, axis=-1, keepdims=True) + RMS_EPS) * nw_ref[...])
    y_ref[0] = _silu(g_ref[0]) * jnp.concatenate(parts, axis=-1)


def gdn_post(o, norm_w, z, tr, gate_col):
    bsz, nh, t, dv = o.shape
    return pl.pallas_call(
        functools.partial(_gdn_post_kernel, nh=nh),
        grid=(bsz, t // tr),
        in_specs=[pl.BlockSpec((1, nh, tr, dv), lambda b, i: (b, 0, i, 0)),
                  pl.BlockSpec((1, dv), lambda b, i: (0, 0)),
                  pl.BlockSpec((1, tr, nh * dv), lambda b, i: (b, i, gate_col))],
        out_specs=pl.BlockSpec((1, tr, nh * dv), lambda b, i: (b, i, 0)),
        out_shape=jax.ShapeDtypeStruct((bsz, t, nh * dv), F32),
        compiler_params=_params("arbitrary", "arbitrary"),
        name="gdn_post",
    )(o, norm_w.reshape(1, dv), z)


def _cross_attn_kernel(q_ref, mk_ref, mv_ref, o_ref):
    dh = q_ref.shape[2] // H_X
    outs = []
    for h in range(H_X):
        cols = slice(h * dh, (h + 1) * dh)
        s = _dot(q_ref[0, :, cols] * (dh ** -0.5), mk_ref[0, :, cols], NT)
        p = jnp.exp(s - jnp.max(s, axis=-1, keepdims=True))
        p = p / jnp.sum(p, axis=-1, keepdims=True)
        outs.append(_dot(p, mv_ref[0, :, cols]))
    o_ref[0] = jnp.concatenate(outs, axis=-1)


def cross_attn(q, mk, mv, tr):
    bsz, t, d = q.shape
    nm = mk.shape[1]
    return pl.pallas_call(
        _cross_attn_kernel,
        grid=(bsz, t // tr),
        in_specs=[pl.BlockSpec((1, tr, d), lambda b, i: (b, i, 0)),
                  pl.BlockSpec((1, nm, d), lambda b, i: (b, 0, 0)),
                  pl.BlockSpec((1, nm, d), lambda b, i: (b, 0, 0))],
        out_specs=pl.BlockSpec((1, tr, d), lambda b, i: (b, i, 0)),
        out_shape=jax.ShapeDtypeStruct((bsz, t, d), F32),
        compiler_params=_params("arbitrary", "arbitrary"),
        name="cross_attn",
    )(q, mk, mv)


def _pad_rows(a, rows, axis=1, front=False):
    pad = [(0, 0)] * a.ndim
    pad[axis] = (rows - a.shape[axis], 0) if front else (0, rows - a.shape[axis])
    return jnp.pad(a, pad)


def _tile(t, pref):
    return pref if t % pref == 0 else t


def _layer(x, t_valid, attn_fn, conv_b_prev, conv_c_prev, s0, mk, mv, lw):
    bsz, t, d = x.shape
    m = bsz * t
    w_b = lw["conv_b_w"].shape[1]
    w_c = lw["conv_c_w"].shape[1] // 3
    nh_c = w_c // DV_C
    w_a = lw["w_a"]
    assert w_a == w_b == w_c and w_a % LANE == 0, "column blocks of z are addressed in units of one group width"
    tm = _tile(m, 512)
    tr = _tile(t, 256)

    z = matmul(x.reshape(m, d), lw["w_in"], tm, lw["w_in"].shape[1] // lw["n_split"]).reshape(bsz, t, -1)
    ya = attn_fn(z)
    yb, tail_b = conv_b(z, _pad_rows(conv_b_prev, SUBLANE, front=True), _pad_rows(lw["conv_b_w"], SUBLANE, axis=0),
                        tr, 4)
    qc, kc, vc = gdn_pre(z, _pad_rows(conv_c_prev, SUBLANE, front=True), _pad_rows(lw["conv_c_w"], SUBLANE, axis=0),
                         tr, 8)
    ab_col = (12 * w_a) // LANE
    if t % GDN_CHUNK == 0:
        o, s_new = gdn_chunk(qc, kc, vc, z, ab_col, lw["alog_lane"], lw["dtb_lane"], s0, _tile(t, 512), t_valid,
                             lw["passes"])
    else:
        tp = GDN_CHUNK
        padt = lambda a: _pad_rows(a, tp, axis=2)
        ab = _pad_rows(z[:, :, 12 * w_a:12 * w_a + LANE], tp, axis=1)
        o, s_new = gdn_chunk(padt(qc), padt(kc), padt(vc), ab, 0, lw["alog_lane"], lw["dtb_lane"], s0, tp, t_valid,
                             lw["passes"])
        o = o[:, :, :t]
    yc = gdn_post(o, lw["norm_w"], z, tr, 11)

    w_out = lw["w_out"]
    x2 = outproj_ln([ya.reshape(m, w_a), yb.reshape(m, w_b), yc.reshape(m, w_c)],
                    [w_out[:w_a], w_out[w_a:w_a + w_b], w_out[w_a + w_b:]],
                    x.reshape(m, d), lw["ln1_g"], lw["ln1_b"], lw["alpha"], tm)
    q = matmul(x2, lw["wq_x"], tm, d).reshape(bsz, t, d)
    ctx = cross_attn(q, mk, mv, tr)
    x3 = outproj_ln([ctx.reshape(m, d)], [lw["wo_x"]], x2, lw["ln2_g"], lw["ln2_b"], lw["alpha"], tm)

    nh_a = w_a // DH_A
    k_new = z[:, :t_valid, w_a:2 * w_a].reshape(bsz, t_valid, nh_a, DH_A)
    v_new = z[:, :t_valid, 2 * w_a:3 * w_a].reshape(bsz, t_valid, nh_a, DH_A)
    tail0 = t - SUBLANE
    conv_b_new = tail_b[:, t_valid - tail0 - (CONV_B - 1):t_valid - tail0]
    conv_c_new = z[:, t_valid - (CONV_C - 1):t_valid, 8 * w_a:11 * w_a]
    return x3.reshape(bsz, t, d), k_new, v_new, conv_b_new, conv_c_new, s_new


def kernel(x_prompt, x_sample, mem_prompt, cache_attn_k, cache_attn_v, cache_mem_k, cache_mem_v, state_conv_b, state_conv_c, state_delta, page_table, w_in, conv_b_w, conv_c_w, gdn_a_log, gdn_dt_bias, gdn_norm_w, w_out, ln1_g, ln1_b, rel_bias, wq_x, wk_x, wv_x, wo_x, ln2_g, ln2_b):
    depth, d, n_in = w_in.shape
    bp, seq, _ = x_prompt.shape
    bs, dec_seq, _ = x_sample.shape
    nh_a = rel_bias.shape[1]
    w_a = nh_a * DH_A
    nh_c = gdn_a_log.shape[1]
    n_mem = mem_prompt.shape[1]
    page = cache_attn_k.shape[2]
    past = page_table.shape[1] * page
    alpha = float((2 * depth) ** 0.25)
    assert n_in == 12 * w_a + 2 * nh_c and dec_seq <= SUBLANE and 2 * nh_c <= LANE
    assert seq >= CONV_C - 1 and dec_seq >= CONV_C - 1 and past % MOBA_BLOCK == 0

    nz = 12 * w_a + 2 * LANE
    n_split = 2
    lane_vec = lambda v: _pad_rows(v.reshape(1, -1).astype(F32), LANE, axis=1)

    bias_prompt = bias_table(rel_bias, (0, MOBA_BLOCK, 2 * MOBA_BLOCK), MOBA_BLOCK, MOBA_BLOCK)
    bias_prompt = bias_prompt.reshape(nh_a, 3, MOBA_BLOCK, MOBA_BLOCK)
    nbf = past // MOBA_BLOCK
    bias_past = bias_table(rel_bias, (past - (nbf - 1) * MOBA_BLOCK, past - (nbf - 2) * MOBA_BLOCK), SUBLANE, MOBA_BLOCK)
    bias_past = bias_past.reshape(nh_a, 2, SUBLANE, MOBA_BLOCK).transpose(1, 0, 2, 3).reshape(2, nh_a * SUBLANE, MOBA_BLOCK)
    bias_own = bias_table(rel_bias, (0,), SUBLANE, LANE).reshape(nh_a * SUBLANE, LANE)

    xp = x_prompt
    xs = _pad_rows(x_sample, SUBLANE)
    outs = [[] for _ in range(12)]
    for l in range(depth):
        lw = dict(
            w_in=_pad_rows(w_in[l], nz, axis=1).astype(BF16), n_split=n_split, w_a=w_a,
            conv_b_w=conv_b_w[l], conv_c_w=conv_c_w[l],
            alog_lane=lane_vec(gdn_a_log[l]), dtb_lane=lane_vec(gdn_dt_bias[l]), norm_w=gdn_norm_w[l],
            w_out=w_out[l].astype(BF16), ln1_g=ln1_g[l], ln1_b=ln1_b[l],
            wq_x=wq_x[l].astype(BF16), wo_x=wo_x[l].astype(BF16), ln2_g=ln2_g[l], ln2_b=ln2_b[l],
            alpha=alpha, passes=3)
        mem2 = mem_prompt.reshape(bp * n_mem, d)
        tmem = _tile(bp * n_mem, 512)
        mk = matmul(mem2, wk_x[l].astype(BF16), tmem, d).reshape(bp, n_mem, d)
        mv = matmul(mem2, wv_x[l].astype(BF16), tmem, d).reshape(bp, n_mem, d)

        xp, k, v, cb, cc, sd = _layer(
            xp, seq, functools.partial(moba_prompt, bias=bias_prompt),
            jnp.zeros((bp, CONV_B - 1, conv_b_w.shape[2]), F32), jnp.zeros((bp, CONV_C - 1, conv_c_w.shape[2]), F32),
            jnp.zeros((bp, nh_c, DK_C, DV_C), F32), mk, mv, lw)
        for lst, val in zip(outs[:2] + outs[4:6] + [outs[6], outs[8], outs[10]],
                            (k, v, mk.reshape(bp, n_mem, H_X, d // H_X), mv.reshape(bp, n_mem, H_X, d // H_X),
                             cb, cc, sd)):
            lst.append(val)

        attn_s = functools.partial(
            moba_sample, cache_k=cache_attn_k[l].reshape(-1, page, w_a), cache_v=cache_attn_v[l].reshape(-1, page, w_a),
            page_table=page_table, bias_past=bias_past, bias_own=bias_own)
        xs, k, v, cb, cc, sd = _layer(
            xs, dec_seq, attn_s, state_conv_b[l], state_conv_c[l], state_delta[l],
            cache_mem_k[l].reshape(bs, n_mem, d), cache_mem_v[l].reshape(bs, n_mem, d), lw)
        for lst, val in zip(outs[2:4] + [outs[7], outs[9], outs[11]], (k, v, cb, cc, sd)):
            lst.append(val)

    kp, vp, ksm, vsm, mkp, mvp, cbp, cbs, ccp, ccs, sdp, sds = (jnp.stack(o) for o in outs)
    return (xp, xs[:, :dec_seq], kp, vp, ksm, vsm, mkp, mvp, cbp, cbs, ccp, ccs, sdp, sds)
```

```python
import functools
import math

import numpy as np
import jax
import jax.numpy as jnp
from jax import lax
from jax.experimental import pallas as pl
from jax.experimental.pallas import tpu as pltpu

F32 = jnp.float32
BF16 = jnp.bfloat16

LANE = 128
SUBLANE = 8
VMEM_LIMIT = 56 * 1024 * 1024

DH_A = 64
MOBA_BLOCK = 256
MOBA_TOPK = 3
N_BUCKETS = 32
MAX_EXACT = 16
MAX_DIST = 128
CONV_B = 3
CONV_C = 4
DK_C = 64
DV_C = 64
GDN_CHUNK = 64
H_X = 4
LN_EPS = 1e-5
RMS_EPS = 1e-6
NEG = -1e30
M_INIT = -3.0e38

NN = ((1,), (0,))
NT = ((1,), (1,))
TN = ((0,), (0,))


def _dot(a, b, dims=NN, passes=1):
    dn = (dims, ((), ()))
    if passes == 6:
        return lax.dot_general(a, b, dn, precision=lax.Precision.HIGHEST, preferred_element_type=F32)
    ah = a.astype(BF16)
    bh = b.astype(BF16)
    out = lax.dot_general(ah, bh, dn, preferred_element_type=F32)
    if passes == 3:
        al = (a - ah.astype(F32)).astype(BF16)
        bl = (b - bh.astype(F32)).astype(BF16)
        out = out + lax.dot_general(ah, bl, dn, preferred_element_type=F32)
        out = out + lax.dot_general(al, bh, dn, preferred_element_type=F32)
    return out


def _silu(x):
    return x * (1.0 / (1.0 + jnp.exp(-x)))


def _params(*sem):
    return pltpu.CompilerParams(dimension_semantics=sem, vmem_limit_bytes=VMEM_LIMIT)


def _bucket_thresholds():
    n = np.arange(0, 4 * MAX_DIST, dtype=np.int32)
    ratio = np.log(np.maximum(n, 1).astype(np.float32) / np.float32(MAX_EXACT)) / np.float32(math.log(MAX_DIST / MAX_EXACT))
    large = MAX_EXACT + (ratio * np.float32(N_BUCKETS - MAX_EXACT)).astype(np.int32)
    bucket = np.where(n < MAX_EXACT, n, np.minimum(large, N_BUCKETS - 1))
    return [int(np.argmax(bucket >= b)) for b in range(N_BUCKETS)]


_THR = _bucket_thresholds()


def _bias_kernel(rel_ref, o_ref, *, bases, rows):
    h = pl.program_id(0)
    cols = o_ref.shape[2]
    for i, base in enumerate(bases):
        dist = (base + lax.broadcasted_iota(jnp.int32, (rows, cols), 0)
                - lax.broadcasted_iota(jnp.int32, (rows, cols), 1))
        val = jnp.full((rows, cols), rel_ref[0, h], F32)
        for b in range(1, N_BUCKETS):
            val = jnp.where(dist >= _THR[b], rel_ref[b, h], val)
        o_ref[0, i * rows:(i + 1) * rows, :] = jnp.where(dist >= 0, val, NEG)


def bias_table(rel_bias, bases, rows, cols):
    nh = rel_bias.shape[1]
    return pl.pallas_call(
        functools.partial(_bias_kernel, bases=tuple(bases), rows=rows),
        grid=(nh,),
        in_specs=[pl.BlockSpec(memory_space=pltpu.SMEM)],
        out_specs=pl.BlockSpec((1, len(bases) * rows, cols), lambda h: (h, 0, 0)),
        out_shape=jax.ShapeDtypeStruct((nh, len(bases) * rows, cols), F32),
        compiler_params=_params("arbitrary"),
        name="bias_table",
    )(rel_bias)


def _mm_kernel(x_ref, w_ref, o_ref):
    o_ref[...] = jnp.dot(x_ref[...].astype(BF16), w_ref[...], preferred_element_type=F32)


def matmul(x, w, tm, tn):
    m, k = x.shape
    n = w.shape[1]
    return pl.pallas_call(
        _mm_kernel,
        grid=(n // tn, m // tm),
        in_specs=[pl.BlockSpec((tm, k), lambda j, i: (i, 0)),
                  pl.BlockSpec((k, tn), lambda j, i: (0, j))],
        out_specs=pl.BlockSpec((tm, tn), lambda j, i: (i, j)),
        out_shape=jax.ShapeDtypeStruct((m, n), F32),
        compiler_params=_params("arbitrary", "arbitrary"),
        name="matmul",
    )(x, w)


def _outproj_ln_kernel(*refs, n, alpha):
    a_refs, w_refs = refs[:n], refs[n:2 * n]
    x_ref, g_ref, b_ref, o_ref = refs[2 * n:]
    y = _dot(a_refs[0][...], w_refs[0][...])
    for a_ref, w_ref in zip(a_refs[1:], w_refs[1:]):
        y = y + _dot(a_ref[...], w_ref[...])
    hid = alpha * x_ref[...] + y
    mu = jnp.mean(hid, axis=-1, keepdims=True)
    cen = hid - mu
    var = jnp.mean(cen * cen, axis=-1, keepdims=True)
    o_ref[...] = cen * lax.rsqrt(var + LN_EPS) * g_ref[...] + b_ref[...]


def outproj_ln(acts, weights, x, g, b, alpha, tm):
    m, d = x.shape
    n = len(acts)
    in_specs = ([pl.BlockSpec((tm, a.shape[1]), lambda i: (i, 0)) for a in acts]
                + [pl.BlockSpec(w.shape, lambda i: (0, 0)) for w in weights]
                + [pl.BlockSpec((tm, d), lambda i: (i, 0)),
                   pl.BlockSpec((1, d), lambda i: (0, 0)),
                   pl.BlockSpec((1, d), lambda i: (0, 0))])
    return pl.pallas_call(
        functools.partial(_outproj_ln_kernel, n=n, alpha=alpha),
        grid=(m // tm,),
        in_specs=in_specs,
        out_specs=pl.BlockSpec((tm, d), lambda i: (i, 0)),
        out_shape=jax.ShapeDtypeStruct((m, d), F32),
        compiler_params=_params("arbitrary"),
        name="outproj_ln",
    )(*acts, *weights, x, g.reshape(1, d), b.reshape(1, d))


def _select_topk(gate, valid, lane, k):
    big = float(1 << 20)
    lane = lane.astype(F32)
    g = jnp.where(valid, gate, NEG)
    live_lane = jnp.where(valid, lane, big)
    sel = jnp.zeros(gate.shape, jnp.int32)
    for _ in range(k):
        m = jnp.max(g, axis=-1, keepdims=True)
        idx = jnp.min(jnp.where(g == m, live_lane, big), axis=-1, keepdims=True)
        hit = lane == idx
        sel = jnp.where(hit, 1, sel)
        live_lane = jnp.where(hit, big, live_lane)
        g = jnp.where(hit, NEG, g)
    return sel


def _moba_prompt_kernel(q_ref, k_ref, v_ref, g_ref, bias_ref, o_ref, kaug, vaug, kmat, s_scr, *, nb, group):
    qi = pl.program_id(2)
    blk = MOBA_BLOCK
    lane = lax.broadcasted_iota(jnp.int32, (blk, LANE), 1)

    @pl.when(qi == 0)
    def _():
        s = k_ref.shape[1]
        kmean = jnp.sum(k_ref[0].reshape(nb, blk, LANE), axis=1) * (1.0 / blk)
        lane_nb = lax.broadcasted_iota(jnp.int32, (nb, LANE), 1)
        for hh in range(2):
            free0 = (1 - hh) * DH_A
            kmat[hh] = jnp.zeros((LANE, LANE), F32)
            kmat[hh, free0:free0 + nb, :] = jnp.where(lane_nb // DH_A == hh, kmean, 0.0)

        def fill(j, carry):
            rows = pl.ds(pl.multiple_of(j * blk, blk), blk)
            kj = k_ref[0, rows, :]
            vj = v_ref[0, rows, :]
            for hh in range(2):
                free0 = (1 - hh) * DH_A
                own = lane // DH_A == hh
                kaug[hh, rows, :] = jnp.where(own, kj, jnp.where(lane == free0 + j, 1.0, 0.0)).astype(BF16)
                vaug[hh, rows, :] = jnp.where(own, vj, 1.0).astype(BF16)
            return carry

        lax.fori_loop(0, s // blk, fill, 0)

    q = q_ref[0] * (DH_A ** -0.5)
    q_aug = []
    for hh in range(2):
        free0 = (1 - hh) * DH_A
        own = lane // DH_A == hh
        qh = jnp.where(own, q, 0.0)
        gate = _dot(qh, kmat[hh], NT, passes=6)
        blk_lane = lane - free0
        sel = _select_topk(gate, (blk_lane >= 0) & (blk_lane < qi), lane, MOBA_TOPK)
        attend = jnp.where(blk_lane == qi, 1, sel)
        masked = jnp.where((blk_lane >= 0) & (blk_lane < nb), 1 - attend, 0)
        q_aug.append(jnp.where(own, q, jnp.where(masked > 0, NEG, 0.0)).astype(BF16))

    ngroups = (qi + group) // group

    def scores(gi, mrun):
        mrun = list(mrun)
        for u in range(group):
            j = gi * group + u
            rows = pl.ds(pl.multiple_of(j * blk, blk), blk)
            for hh in range(2):
                s = _dot(q_aug[hh], kaug[hh, rows, :], NT) + bias_ref[hh, jnp.clip(qi - j, 0, 2)]
                s_scr[hh, j] = s
                mrun[hh] = jnp.maximum(mrun[hh], jnp.maximum(s[:, :LANE], s[:, LANE:]))
        return tuple(mrun)

    mrun = lax.fori_loop(0, ngroups, scores, (jnp.full((blk, LANE), M_INIT, F32),) * 2)
    m = [jnp.max(mr, axis=-1, keepdims=True) for mr in mrun]

    def weighted(gi, acc):
        acc = list(acc)
        for u in range(group):
            j = gi * group + u
            rows = pl.ds(pl.multiple_of(j * blk, blk), blk)
            for hh in range(2):
                acc[hh] = acc[hh] + _dot(jnp.exp(s_scr[hh, j] - m[hh]), vaug[hh, rows, :])
        return tuple(acc)

    acc = lax.fori_loop(0, ngroups, weighted, (jnp.zeros((blk, LANE), F32),) * 2)
    outs = [a / pltpu.roll(a, DH_A, axis=1) for a in acc]
    o_ref[0] = _silu(g_ref[0]) * jnp.where(lane < DH_A, outs[0], outs[1])


def moba_prompt(z, bias):
    bsz, s, _ = z.shape
    nh = bias.shape[0]
    npair = nh // 2
    nb = s // MOBA_BLOCK
    assert s % MOBA_BLOCK == 0 and nb <= DH_A and 2 * DH_A == LANE
    blk = MOBA_BLOCK
    group = next(g for g in (4, 2, 1) if nb % g == 0)
    return pl.pallas_call(
        functools.partial(_moba_prompt_kernel, nb=nb, group=group),
        grid=(bsz, npair, nb),
        in_specs=[pl.BlockSpec((1, blk, LANE), lambda b, p, i: (b, i, p)),
                  pl.BlockSpec((1, s, LANE), lambda b, p, i: (b, 0, npair + p)),
                  pl.BlockSpec((1, s, LANE), lambda b, p, i: (b, 0, 2 * npair + p)),
                  pl.BlockSpec((1, blk, LANE), lambda b, p, i: (b, i, 3 * npair + p)),
                  pl.BlockSpec((2, 3, blk, blk), lambda b, p, i: (p, 0, 0, 0))],
        out_specs=pl.BlockSpec((1, blk, LANE), lambda b, p, i: (b, i, p)),
        out_shape=jax.ShapeDtypeStruct((bsz, s, nh * DH_A), F32),
        scratch_shapes=[pltpu.VMEM((2, s, LANE), BF16), pltpu.VMEM((2, s, LANE), BF16),
                        pltpu.VMEM((2, LANE, LANE), F32), pltpu.VMEM((2, nb, blk, blk), F32)],
        compiler_params=_params("arbitrary", "arbitrary", "arbitrary"),
        name="moba_prompt",
    )(z, z, z, z, bias)


def _moba_sample_kernel(pt_ref, q_ref, kn_ref, vn_ref, g_ref, ka_ref, kb_ref, va_ref, vb_ref,
                        bias_past_ref, bias_own_ref, o_ref, qexp, acc_scr, m_all, l_all, gate_all, *, nb, nh):
    del pt_ref
    j = pl.program_id(1)
    t8 = SUBLANE
    rows = nh * t8
    width = nh * DH_A
    page = ka_ref.shape[2]
    row_head = lax.broadcasted_iota(jnp.int32, (rows, width), 0) // t8
    lane_head = lax.broadcasted_iota(jnp.int32, (rows, width), 1) // DH_A
    head_mask = row_head == lane_head
    lane = lax.broadcasted_iota(jnp.int32, (rows, LANE), 1)

    @pl.when(j == 0)
    def _():
        q8 = q_ref[0] * (DH_A ** -0.5)
        qexp[...] = jnp.where(head_mask, jnp.concatenate([q8] * nh, axis=0), 0.0)
        m_all[...] = jnp.zeros((rows, LANE), F32)
        l_all[...] = jnp.zeros((rows, LANE), F32)
        gate_all[...] = jnp.zeros((rows, LANE), F32)

    qe = qexp[...]
    kt_a, kt_b = ka_ref[0], kb_ref[0]
    ksum = jnp.sum(kt_a + kt_b, axis=-1, keepdims=True) * (1.0 / MOBA_BLOCK)
    gate = _dot(qe, jnp.broadcast_to(ksum, (width, LANE)), NN, passes=6)
    bias = bias_past_ref[jnp.where(j == nb - 1, 0, 1)]
    s_a = _dot(qe, kt_a) + bias[:, :page]
    s_b = _dot(qe, kt_b) + bias[:, page:]
    m_j = jnp.maximum(jnp.max(s_a, axis=-1, keepdims=True), jnp.max(s_b, axis=-1, keepdims=True))
    p_a = jnp.exp(s_a - m_j)
    p_b = jnp.exp(s_b - m_j)
    l_j = jnp.sum(p_a + p_b, axis=-1, keepdims=True)
    acc_scr[j] = _dot(p_a, va_ref[0], NT) + _dot(p_b, vb_ref[0], NT)
    m_all[...] = jnp.where(lane == j, m_j, m_all[...])
    l_all[...] = jnp.where(lane == j, l_j, l_all[...])
    gate_all[...] = jnp.where(lane == j, gate, gate_all[...])

    @pl.when(j == nb - 1)
    def _():
        pad = jnp.zeros((LANE - t8, width), F32)
        s_o = _dot(qe, jnp.concatenate([kn_ref[0], pad], axis=0), NT) + bias_own_ref[...]
        m_o = jnp.max(s_o, axis=-1, keepdims=True)
        p_o = jnp.exp(s_o - m_o)
        l_o = jnp.sum(p_o, axis=-1, keepdims=True)
        acc_o = _dot(p_o, jnp.concatenate([vn_ref[0], pad], axis=0))
        sel = _select_topk(gate_all[...], lane < nb, lane, min(MOBA_TOPK, nb)) > 0
        mm = m_all[...]
        m_tot = jnp.maximum(m_o, jnp.max(jnp.where(sel, mm, M_INIT), axis=-1, keepdims=True))
        w_all = jnp.where(sel, jnp.exp(jnp.where(sel, mm, m_tot) - m_tot), 0.0)
        w_o = jnp.exp(m_o - m_tot)
        l_tot = w_o * l_o + jnp.sum(w_all * l_all[...], axis=-1, keepdims=True)
        tot = w_o * acc_o
        for jb in range(nb):
            tot = tot + w_all[:, jb:jb + 1] * acc_scr[jb]
        tot = jnp.where(head_mask, tot / l_tot, 0.0)
        out8 = tot[0:t8]
        for h in range(1, nh):
            out8 = out8 + tot[h * t8:(h + 1) * t8]
        o_ref[0] = _silu(g_ref[0]) * out8


def moba_sample(z, cache_kt, cache_vt, page0, page_table, bias_past, bias_own):
    bs, t8, _ = z.shape
    width, page = cache_kt.shape[1:]
    nh = width // DH_A
    n_pages = page_table.shape[1]
    assert t8 == SUBLANE and MOBA_BLOCK == 2 * page and (n_pages * page) % MOBA_BLOCK == 0
    nb = n_pages * page // MOBA_BLOCK
    assert nb <= LANE
    rows = nh * t8
    zspec = lambda c: pl.BlockSpec((1, t8, width), lambda b, j, pt: (b, 0, c))
    pspec = lambda o: pl.BlockSpec((1, width, page), lambda b, j, pt: (page0 + pt[b, 2 * j + o], 0, 0))
    grid_spec = pltpu.PrefetchScalarGridSpec(
        num_scalar_prefetch=1,
        grid=(bs, nb),
        in_specs=[zspec(0), zspec(1), zspec(2), zspec(3), pspec(0), pspec(1), pspec(0), pspec(1),
                  pl.BlockSpec((2, rows, MOBA_BLOCK), lambda b, j, pt: (0, 0, 0)),
                  pl.BlockSpec((rows, LANE), lambda b, j, pt: (0, 0))],
        out_specs=pl.BlockSpec((1, t8, width), lambda b, j, pt: (b, 0, 0)),
        scratch_shapes=[pltpu.VMEM((rows, width), F32), pltpu.VMEM((nb, rows, width), F32),
                        pltpu.VMEM((rows, LANE), F32), pltpu.VMEM((rows, LANE), F32),
                        pltpu.VMEM((rows, LANE), F32)])
    return pl.pallas_call(
        functools.partial(_moba_sample_kernel, nb=nb, nh=nh),
        grid_spec=grid_spec,
        out_shape=jax.ShapeDtypeStruct((bs, t8, width), F32),
        compiler_params=_params("arbitrary", "arbitrary"),
        name="moba_sample",
    )(page_table, z, z, z, z, cache_kt, cache_kt, cache_vt, cache_vt, bias_past, bias_own)


def _conv_b_kernel(b_ref, c_ref, h_ref, g_ref, prev_ref, w_ref, y_ref, tail_ref, buf):
    r = c_ref.shape[1]

    @pl.when(pl.program_id(1) == 0)
    def _():
        buf[0:SUBLANE, :] = prev_ref[0]

    u = c_ref[0] * h_ref[0]
    buf[SUBLANE:SUBLANE + r, :] = u
    conv = u * w_ref[CONV_B - 1:CONV_B, :]
    for jtap in range(CONV_B - 1):
        off = SUBLANE - (CONV_B - 1) + jtap
        conv = conv + buf[off:off + r, :] * w_ref[jtap:jtap + 1, :]
    y_ref[0] = _silu(g_ref[0]) * (b_ref[0] * conv)
    tail_ref[0] = u[r - SUBLANE:, :]
    buf[0:SUBLANE, :] = u[r - SUBLANE:, :]


def conv_b(z, prev8, w8, tr, col0):
    bsz, t, _ = z.shape
    width = prev8.shape[2]
    zspec = lambda c: pl.BlockSpec((1, tr, width), lambda b, i: (b, i, c))
    return pl.pallas_call(
        _conv_b_kernel,
        grid=(bsz, t // tr),
        in_specs=[zspec(col0), zspec(col0 + 1), zspec(col0 + 2), zspec(col0 + 3),
                  pl.BlockSpec((1, SUBLANE, width), lambda b, i: (b, 0, 0)),
                  pl.BlockSpec((SUBLANE, width), lambda b, i: (0, 0))],
        out_specs=[pl.BlockSpec((1, tr, width), lambda b, i: (b, i, 0)),
                   pl.BlockSpec((1, SUBLANE, width), lambda b, i: (b, 0, 0))],
        out_shape=[jax.ShapeDtypeStruct((bsz, t, width), F32),
                   jax.ShapeDtypeStruct((bsz, SUBLANE, width), F32)],
        scratch_shapes=[pltpu.VMEM((tr + SUBLANE, width), F32)],
        compiler_params=_params("arbitrary", "arbitrary"),
        name="conv_b",
    )(z, z, z, z, prev8, w8)


def _gdn_pre_kernel(xq_ref, xk_ref, xv_ref, pq_ref, pk_ref, pv_ref, w_ref, q_ref, k_ref, v_ref, buf, *, nh):
    r = xq_ref.shape[1]
    width = xq_ref.shape[2]
    first = pl.program_id(1) == 0
    for idx, (x_ref, p_ref, o_ref) in enumerate(((xq_ref, pq_ref, q_ref), (xk_ref, pk_ref, k_ref),
                                                 (xv_ref, pv_ref, v_ref))):
        @pl.when(first)
        def _(idx=idx, p_ref=p_ref):
            buf[idx, 0:SUBLANE, :] = p_ref[0]

        x = x_ref[0]
        buf[idx, SUBLANE:SUBLANE + r, :] = x
        wcol = slice(idx * width, (idx + 1) * width)
        conv = x * w_ref[CONV_C - 1:CONV_C, wcol]
        for jtap in range(CONV_C - 1):
            off = SUBLANE - (CONV_C - 1) + jtap
            conv = conv + buf[idx, off:off + r, :] * w_ref[jtap:jtap + 1, wcol]
        buf[idx, 0:SUBLANE, :] = x[r - SUBLANE:, :]
        act = _silu(conv)
        for h in range(nh):
            a = act[:, h * DK_C:(h + 1) * DK_C]
            if idx < 2:
                a = a * lax.rsqrt(jnp.sum(a * a, axis=-1, keepdims=True) + RMS_EPS)
            o_ref[0, h] = a


def gdn_pre(z, prev8, w8, tr, col0):
    bsz, t, _ = z.shape
    width = prev8.shape[2] // 3
    nh = width // DK_C
    zspec = lambda c: pl.BlockSpec((1, tr, width), lambda b, i: (b, i, c))
    pspec = lambda c: pl.BlockSpec((1, SUBLANE, width), lambda b, i: (b, 0, c))
    ospec = pl.BlockSpec((1, nh, tr, DK_C), lambda b, i: (b, 0, i, 0))
    oshape = jax.ShapeDtypeStruct((bsz, nh, t, DK_C), F32)
    return pl.pallas_call(
        functools.partial(_gdn_pre_kernel, nh=nh),
        grid=(bsz, t // tr),
        in_specs=[zspec(col0), zspec(col0 + 1), zspec(col0 + 2), pspec(0), pspec(1), pspec(2),
                  pl.BlockSpec((SUBLANE, 3 * width), lambda b, i: (0, 0))],
        out_specs=[ospec, ospec, ospec],
        out_shape=[oshape, oshape, oshape],
        scratch_shapes=[pltpu.VMEM((3, tr + SUBLANE, width), F32)],
        compiler_params=_params("arbitrary", "arbitrary"),
        name="gdn_pre",
    )(z, z, z, prev8, prev8, prev8, w8)


def _gdn_chunk_kernel(q_ref, k_ref, v_ref, ab_ref, alog_ref, dtb_ref, s0_ref, o_ref, sout_ref, s_scr,
                      *, nh, t_valid, passes):
    c = GDN_CHUNK
    ti = pl.program_id(1)
    tc = q_ref.shape[2]

    @pl.when(ti == 0)
    def _():
        s_scr[...] = s0_ref[0]

    row = lax.broadcasted_iota(jnp.int32, (c, c), 0)
    col = lax.broadcasted_iota(jnp.int32, (c, c), 1)
    tri = row >= col
    strict = row > col
    eye = jnp.where(row == col, 1.0, 0.0)
    tril_ones = jnp.where(tri, 1.0, 0.0)
    lane_c = lax.broadcasted_iota(jnp.int32, (c, LANE), 1)
    neg_a = -jnp.exp(alog_ref[...])
    dtb = dtb_ref[...]

    def chunk(ci, carry):
        r0 = pl.multiple_of(ci * c, c)
        ab = ab_ref[0, pl.ds(r0, c), :]
        g_all = neg_a * (jnp.maximum(ab + dtb, 0.0) + jnp.log(1.0 + jnp.exp(-jnp.abs(ab + dtb))))
        beta_all = 1.0 / (1.0 + jnp.exp(-ab))
        if t_valid is not None:
            live = (ti * tc + r0 + lax.broadcasted_iota(jnp.int32, (c, LANE), 0)) < t_valid
            g_all = jnp.where(live, g_all, 0.0)
            beta_all = jnp.where(live, beta_all, 0.0)
        gc_all = _dot(tril_ones, g_all, NN, passes=6)
        for h in range(nh):
            gcol = gc_all[:, h:h + 1]
            bcol = beta_all[:, nh + h:nh + h + 1]
            grow = _dot(jnp.where(lane_c == h, 1.0, 0.0), gc_all, NT, passes=6)
            decay = jnp.where(tri, jnp.exp(jnp.where(tri, gcol - grow, 0.0)), 0.0)
            qh = q_ref[0, h, pl.ds(r0, c), :] * (DK_C ** -0.5)
            kh = k_ref[0, h, pl.ds(r0, c), :]
            vh = v_ref[0, h, pl.ds(r0, c), :]
            kbeta = kh * bcol
            a_mat = jnp.where(strict, _dot(kbeta, kh, NT, passes) * decay, 0.0)
            tinv = eye - a_mat
            power = a_mat
            for _ in range(int(math.log2(c)) - 1):
                power = _dot(power, power, NN, passes)
                tinv = tinv + _dot(tinv, power, NN, passes)
            egc = jnp.exp(gcol)
            u = _dot(tinv, vh * bcol, NN, passes)
            w = _dot(tinv, kbeta * egc, NN, passes)
            s_h = s_scr[h]
            v_new = u - _dot(w, s_h, NN, passes)
            attn = _dot(qh, kh, NT, passes) * decay
            o_ref[0, h, pl.ds(r0, c), :] = _dot(qh * egc, s_h, NN, passes) + _dot(attn, v_new, NN, passes)
            g_last = gc_all[c - 1:c, h:h + 1]
            s_scr[h] = s_h * jnp.exp(g_last) + _dot(kh * jnp.exp(g_last - gcol), v_new, TN, passes)
        return carry

    lax.fori_loop(0, tc // c, chunk, 0)

    @pl.when(ti == pl.num_programs(1) - 1)
    def _():
        sout_ref[0] = s_scr[...]


def gdn_chunk(q, k, v, ab, ab_col, alog_lane, dtb_lane, s0, tc, t_valid, passes):
    bsz, nh, t, _ = q.shape
    assert t % tc == 0 and tc % GDN_CHUNK == 0
    qspec = pl.BlockSpec((1, nh, tc, DK_C), lambda b, i: (b, 0, i, 0))
    sspec = pl.BlockSpec((1, nh, DK_C, DV_C), lambda b, i: (b, 0, 0, 0))
    vec = pl.BlockSpec((1, LANE), lambda b, i: (0, 0))
    return pl.pallas_call(
        functools.partial(_gdn_chunk_kernel, nh=nh, t_valid=None if t_valid == t else t_valid, passes=passes),
        grid=(bsz, t // tc),
        in_specs=[qspec, qspec, qspec, pl.BlockSpec((1, tc, LANE), lambda b, i: (b, i, ab_col)), vec, vec, sspec],
        out_specs=[qspec, sspec],
        out_shape=[jax.ShapeDtypeStruct((bsz, nh, t, DV_C), F32),
                   jax.ShapeDtypeStruct((bsz, nh, DK_C, DV_C), F32)],
        scratch_shapes=[pltpu.VMEM((nh, DK_C, DV_C), F32)],
        compiler_params=_params("arbitrary", "arbitrary"),
        name="gdn_chunk",
    )(q, k, v, ab, alog_lane, dtb_lane, s0)


def _gdn_post_kernel(o_ref, nw_ref, g_ref, y_ref, *, nh):
    parts = []
    for h in range(nh):
        o = o_ref[0, h]
        parts.append(o * lax.rsqrt(jnp.mean(o * o, axis=-1, keepdims=True) + RMS_EPS) * nw_ref[...])
    y_ref[0] = _silu(g_ref[0]) * jnp.concatenate(parts, axis=-1)


def gdn_post(o, norm_w, z, tr, gate_col):
    bsz, nh, t, dv = o.shape
    return pl.pallas_call(
        functools.partial(_gdn_post_kernel, nh=nh),
        grid=(bsz, t // tr),
        in_specs=[pl.BlockSpec((1, nh, tr, dv), lambda b, i: (b, 0, i, 0)),
                  pl.BlockSpec((1, dv), lambda b, i: (0, 0)),
                  pl.BlockSpec((1, tr, nh * dv), lambda b, i: (b, i, gate_col))],
        out_specs=pl.BlockSpec((1, tr, nh * dv), lambda b, i: (b, i, 0)),
        out_shape=jax.ShapeDtypeStruct((bsz, t, nh * dv), F32),
        compiler_params=_params("arbitrary", "arbitrary"),
        name="gdn_post",
    )(o, norm_w.reshape(1, dv), z)


def _cross_attn_kernel(q_ref, mk_ref, mv_ref, o_ref):
    dh = q_ref.shape[2] // H_X
    outs = []
    for h in range(H_X):
        cols = slice(h * dh, (h + 1) * dh)
        s = _dot(q_ref[0, :, cols] * (dh ** -0.5), mk_ref[0, :, cols], NT)
        p = jnp.exp(s - jnp.max(s, axis=-1, keepdims=True))
        p = p / jnp.sum(p, axis=-1, keepdims=True)
        outs.append(_dot(p, mv_ref[0, :, cols]))
    o_ref[0] = jnp.concatenate(outs, axis=-1)


def cross_attn(q, mk, mv, tr):
    bsz, t, d = q.shape
    nm = mk.shape[1]
    return pl.pallas_call(
        _cross_attn_kernel,
        grid=(bsz, t // tr),
        in_specs=[pl.BlockSpec((1, tr, d), lambda b, i: (b, i, 0)),
                  pl.BlockSpec((1, nm, d), lambda b, i: (b, 0, 0)),
                  pl.BlockSpec((1, nm, d), lambda b, i: (b, 0, 0))],
        out_specs=pl.BlockSpec((1, tr, d), lambda b, i: (b, i, 0)),
        out_shape=jax.ShapeDtypeStruct((bsz, t, d), F32),
        compiler_params=_params("arbitrary", "arbitrary"),
        name="cross_attn",
    )(q, mk, mv)


def _pad_rows(a, rows, axis=1, front=False):
    pad = [(0, 0)] * a.ndim
    pad[axis] = (rows - a.shape[axis], 0) if front else (0, rows - a.shape[axis])
    return jnp.pad(a, pad)


def _tile(t, pref):
    return pref if t % pref == 0 else t


def _layer(x, t_valid, attn_fn, conv_b_prev, conv_c_prev, s0, mk, mv, lw):
    bsz, t, d = x.shape
    m = bsz * t
    w_b = lw["conv_b_w"].shape[1]
    w_c = lw["conv_c_w"].shape[1] // 3
    nh_c = w_c // DV_C
    w_a = lw["w_a"]
    assert w_a == w_b == w_c and w_a % LANE == 0, "column blocks of z are addressed in units of one group width"
    tm = _tile(m, 512)
    tr = _tile(t, 256)

    z = matmul(x.reshape(m, d), lw["w_in"], tm, lw["w_in"].shape[1] // lw["n_split"]).reshape(bsz, t, -1)
    ya = attn_fn(z)
    yb, tail_b = conv_b(z, _pad_rows(conv_b_prev, SUBLANE, front=True), _pad_rows(lw["conv_b_w"], SUBLANE, axis=0),
                        tr, 4)
    qc, kc, vc = gdn_pre(z, _pad_rows(conv_c_prev, SUBLANE, front=True), _pad_rows(lw["conv_c_w"], SUBLANE, axis=0),
                         tr, 8)
    ab_col = (12 * w_a) // LANE
    if t % GDN_CHUNK == 0:
        o, s_new = gdn_chunk(qc, kc, vc, z, ab_col, lw["alog_lane"], lw["dtb_lane"], s0, _tile(t, 512), t_valid,
                             lw["passes"])
    else:
        tp = GDN_CHUNK
        padt = lambda a: _pad_rows(a, tp, axis=2)
        ab = _pad_rows(z[:, :, 12 * w_a:12 * w_a + LANE], tp, axis=1)
        o, s_new = gdn_chunk(padt(qc), padt(kc), padt(vc), ab, 0, lw["alog_lane"], lw["dtb_lane"], s0, tp, t_valid,
                             lw["passes"])
        o = o[:, :, :t]
    yc = gdn_post(o, lw["norm_w"], z, tr, 11)

    w_out = lw["w_out"]
    x2 = outproj_ln([ya.reshape(m, w_a), yb.reshape(m, w_b), yc.reshape(m, w_c)],
                    [w_out[:w_a], w_out[w_a:w_a + w_b], w_out[w_a + w_b:]],
                    x.reshape(m, d), lw["ln1_g"], lw["ln1_b"], lw["alpha"], tm)
    q = matmul(x2, lw["wq_x"], tm, d).reshape(bsz, t, d)
    ctx = cross_attn(q, mk, mv, tr)
    x3 = outproj_ln([ctx.reshape(m, d)], [lw["wo_x"]], x2, lw["ln2_g"], lw["ln2_b"], lw["alpha"], tm)

    nh_a = w_a // DH_A
    k_new = z[:, :t_valid, w_a:2 * w_a].reshape(bsz, t_valid, nh_a, DH_A)
    v_new = z[:, :t_valid, 2 * w_a:3 * w_a].reshape(bsz, t_valid, nh_a, DH_A)
    tail0 = t - SUBLANE
    conv_b_new = tail_b[:, t_valid - tail0 - (CONV_B - 1):t_valid - tail0]
    conv_c_new = z[:, t_valid - (CONV_C - 1):t_valid, 8 * w_a:11 * w_a]
    return x3.reshape(bsz, t, d), k_new, v_new, conv_b_new, conv_c_new, s_new


def kernel(x_prompt, x_sample, mem_prompt, cache_attn_k, cache_attn_v, cache_mem_k, cache_mem_v, state_conv_b, state_conv_c, state_delta, page_table, w_in, conv_b_w, conv_c_w, gdn_a_log, gdn_dt_bias, gdn_norm_w, w_out, ln1_g, ln1_b, rel_bias, wq_x, wk_x, wv_x, wo_x, ln2_g, ln2_b):
    depth, d, n_in = w_in.shape
    bp, seq, _ = x_prompt.shape
    bs, dec_seq, _ = x_sample.shape
    nh_a = rel_bias.shape[1]
    w_a = nh_a * DH_A
    nh_c = gdn_a_log.shape[1]
    n_mem = mem_prompt.shape[1]
    page = cache_attn_k.shape[2]
    past = page_table.shape[1] * page
    alpha = float((2 * depth) ** 0.25)
    assert n_in == 12 * w_a + 2 * nh_c and dec_seq <= SUBLANE and 2 * nh_c <= LANE
    assert seq >= CONV_C - 1 and dec_seq >= CONV_C - 1 and past % MOBA_BLOCK == 0

    nz = 12 * w_a + 2 * LANE
    n_split = 2
    lane_vec = lambda v: _pad_rows(v.reshape(1, -1).astype(F32), LANE, axis=1)

    bias_prompt = bias_table(rel_bias, (0, MOBA_BLOCK, 2 * MOBA_BLOCK), MOBA_BLOCK, MOBA_BLOCK)
    bias_prompt = bias_prompt.reshape(nh_a, 3, MOBA_BLOCK, MOBA_BLOCK)
    nbf = past // MOBA_BLOCK
    bias_past = bias_table(rel_bias, (past - (nbf - 1) * MOBA_BLOCK, past - (nbf - 2) * MOBA_BLOCK), SUBLANE, MOBA_BLOCK)
    bias_past = bias_past.reshape(nh_a, 2, SUBLANE, MOBA_BLOCK).transpose(1, 0, 2, 3).reshape(2, nh_a * SUBLANE, MOBA_BLOCK)
    bias_own = bias_table(rel_bias, (0,), SUBLANE, LANE).reshape(nh_a * SUBLANE, LANE)

    n_phys = cache_attn_k.shape[1]
    cache_kt = cache_attn_k.transpose(0, 1, 3, 4, 2).reshape(depth * n_phys, w_a, page)
    cache_vt = cache_attn_v.transpose(0, 1, 3, 4, 2).reshape(depth * n_phys, w_a, page)

    xp = x_prompt
    xs = _pad_rows(x_sample, SUBLANE)
    outs = [[] for _ in range(12)]
    for l in range(depth):
        lw = dict(
            w_in=_pad_rows(w_in[l], nz, axis=1).astype(BF16), n_split=n_split, w_a=w_a,
            conv_b_w=conv_b_w[l], conv_c_w=conv_c_w[l],
            alog_lane=lane_vec(gdn_a_log[l]), dtb_lane=lane_vec(gdn_dt_bias[l]), norm_w=gdn_norm_w[l],
            w_out=w_out[l].astype(BF16), ln1_g=ln1_g[l], ln1_b=ln1_b[l],
            wq_x=wq_x[l].astype(BF16), wo_x=wo_x[l].astype(BF16), ln2_g=ln2_g[l], ln2_b=ln2_b[l],
            alpha=alpha, passes=3)
        mem2 = mem_prompt.reshape(bp * n_mem, d)
        tmem = _tile(bp * n_mem, 512)
        mk = matmul(mem2, wk_x[l].astype(BF16), tmem, d).reshape(bp, n_mem, d)
        mv = matmul(mem2, wv_x[l].astype(BF16), tmem, d).reshape(bp, n_mem, d)

        xp, k, v, cb, cc, sd = _layer(
            xp, seq, functools.partial(moba_prompt, bias=bias_prompt),
            jnp.zeros((bp, CONV_B - 1, conv_b_w.shape[2]), F32), jnp.zeros((bp, CONV_C - 1, conv_c_w.shape[2]), F32),
            jnp.zeros((bp, nh_c, DK_C, DV_C), F32), mk, mv, lw)
        for lst, val in zip(outs[:2] + outs[4:6] + [outs[6], outs[8], outs[10]],
                            (k, v, mk.reshape(bp, n_mem, H_X, d // H_X), mv.reshape(bp, n_mem, H_X, d // H_X),
                             cb, cc, sd)):
            lst.append(val)

        attn_s = functools.partial(
            moba_sample, cache_kt=cache_kt, cache_vt=cache_vt, page0=l * n_phys,
            page_table=page_table, bias_past=bias_past, bias_own=bias_own)
        xs, k, v, cb, cc, sd = _layer(
            xs, dec_seq, attn_s, state_conv_b[l], state_conv_c[l], state_delta[l],
            cache_mem_k[l].reshape(bs, n_mem, d), cache_mem_v[l].reshape(bs, n_mem, d), lw)
        for lst, val in zip(outs[2:4] + [outs[7], outs[9], outs[11]], (k, v, cb, cc, sd)):
            lst.append(val)

    kp, vp, ksm, vsm, mkp, mvp, cbp, cbs, ccp, ccs, sdp, sds = (jnp.stack(o) for o in outs)
    return (xp, xs[:, :dec_seq], kp, vp, ksm, vsm, mkp, mvp, cbp, cbs, ccp, ccs, sdp, sds)
```

```python
import functools
import math

import numpy as np
import jax
import jax.numpy as jnp
from jax import lax
from jax.experimental import pallas as pl
from jax.experimental.pallas import tpu as pltpu

F32 = jnp.float32
BF16 = jnp.bfloat16

LANE = 128
SUBLANE = 8
VMEM_LIMIT = 56 * 1024 * 1024

DH_A = 64
MOBA_BLOCK = 256
MOBA_TOPK = 3
N_BUCKETS = 32
MAX_EXACT = 16
MAX_DIST = 128
CONV_B = 3
CONV_C = 4
DK_C = 64
DV_C = 64
GDN_CHUNK = 64
H_X = 4
LN_EPS = 1e-5
RMS_EPS = 1e-6
NEG = -1e30
M_INIT = -3.0e38

NN = ((1,), (0,))
NT = ((1,), (1,))
TN = ((0,), (0,))


def _dot(a, b, dims=NN, exact=False):
    dn = (dims, ((), ()))
    if exact:
        return lax.dot_general(a, b, dn, precision=lax.Precision.HIGHEST, preferred_element_type=F32)
    return lax.dot_general(a.astype(BF16), b.astype(BF16), dn, preferred_element_type=F32)


def _split(x):
    hi = x.astype(BF16)
    return hi, (x - hi.astype(F32)).astype(BF16)


def _dot_split(a, b, dims=NN):
    dn = (dims, ((), ()))
    out = lax.dot_general(a[0], b[0], dn, preferred_element_type=F32)
    out = out + lax.dot_general(a[0], b[1], dn, preferred_element_type=F32)
    return out + lax.dot_general(a[1], b[0], dn, preferred_element_type=F32)


def _silu(x):
    return x * (1.0 / (1.0 + jnp.exp(-x)))


def _params(*sem):
    return pltpu.CompilerParams(dimension_semantics=sem, vmem_limit_bytes=VMEM_LIMIT)


def _bucket_thresholds():
    n = np.arange(0, 4 * MAX_DIST, dtype=np.int32)
    ratio = np.log(np.maximum(n, 1).astype(np.float32) / np.float32(MAX_EXACT)) / np.float32(math.log(MAX_DIST / MAX_EXACT))
    large = MAX_EXACT + (ratio * np.float32(N_BUCKETS - MAX_EXACT)).astype(np.int32)
    bucket = np.where(n < MAX_EXACT, n, np.minimum(large, N_BUCKETS - 1))
    return [int(np.argmax(bucket >= b)) for b in range(N_BUCKETS)]


_THR = _bucket_thresholds()


def _bias_kernel(rel_ref, o_ref, *, bases, rows):
    h = pl.program_id(0)
    cols = o_ref.shape[2]
    for i, base in enumerate(bases):
        dist = (base + lax.broadcasted_iota(jnp.int32, (rows, cols), 0)
                - lax.broadcasted_iota(jnp.int32, (rows, cols), 1))
        val = jnp.full((rows, cols), rel_ref[0, h], F32)
        for b in range(1, N_BUCKETS):
            val = jnp.where(dist >= _THR[b], rel_ref[b, h], val)
        o_ref[0, i * rows:(i + 1) * rows, :] = jnp.where(dist >= 0, val, NEG)


def bias_table(rel_bias, bases, rows, cols):
    nh = rel_bias.shape[1]
    return pl.pallas_call(
        functools.partial(_bias_kernel, bases=tuple(bases), rows=rows),
        grid=(nh,),
        in_specs=[pl.BlockSpec(memory_space=pltpu.SMEM)],
        out_specs=pl.BlockSpec((1, len(bases) * rows, cols), lambda h: (h, 0, 0)),
        out_shape=jax.ShapeDtypeStruct((nh, len(bases) * rows, cols), F32),
        compiler_params=_params("arbitrary"),
        name="bias_table",
    )(rel_bias)


def _mm_kernel(x_ref, w_ref, o_ref):
    o_ref[...] = jnp.dot(x_ref[...].astype(BF16), w_ref[...], preferred_element_type=F32)


def matmul(x, w, tm, tn):
    m, k = x.shape
    n = w.shape[1]
    return pl.pallas_call(
        _mm_kernel,
        grid=(n // tn, m // tm),
        in_specs=[pl.BlockSpec((tm, k), lambda j, i: (i, 0)),
                  pl.BlockSpec((k, tn), lambda j, i: (0, j))],
        out_specs=pl.BlockSpec((tm, tn), lambda j, i: (i, j)),
        out_shape=jax.ShapeDtypeStruct((m, n), F32),
        compiler_params=_params("arbitrary", "arbitrary"),
        name="matmul",
    )(x, w)


def _outproj_ln_kernel(*refs, n, alpha):
    a_refs, w_refs = refs[:n], refs[n:2 * n]
    x_ref, g_ref, b_ref, o_ref = refs[2 * n:]
    y = _dot(a_refs[0][...], w_refs[0][...])
    for a_ref, w_ref in zip(a_refs[1:], w_refs[1:]):
        y = y + _dot(a_ref[...], w_ref[...])
    hid = alpha * x_ref[...] + y
    mu = jnp.mean(hid, axis=-1, keepdims=True)
    cen = hid - mu
    var = jnp.mean(cen * cen, axis=-1, keepdims=True)
    o_ref[...] = cen * lax.rsqrt(var + LN_EPS) * g_ref[...] + b_ref[...]


def outproj_ln(acts, weights, x, g, b, alpha, tm):
    m, d = x.shape
    n = len(acts)
    in_specs = ([pl.BlockSpec((tm, a.shape[1]), lambda i: (i, 0)) for a in acts]
                + [pl.BlockSpec(w.shape, lambda i: (0, 0)) for w in weights]
                + [pl.BlockSpec((tm, d), lambda i: (i, 0)),
                   pl.BlockSpec((1, d), lambda i: (0, 0)),
                   pl.BlockSpec((1, d), lambda i: (0, 0))])
    return pl.pallas_call(
        functools.partial(_outproj_ln_kernel, n=n, alpha=alpha),
        grid=(m // tm,),
        in_specs=in_specs,
        out_specs=pl.BlockSpec((tm, d), lambda i: (i, 0)),
        out_shape=jax.ShapeDtypeStruct((m, d), F32),
        compiler_params=_params("arbitrary"),
        name="outproj_ln",
    )(*acts, *weights, x, g.reshape(1, d), b.reshape(1, d))


def _select_topk(gate, valid, pos, k, axis=-1):
    big = float(1 << 20)
    pos = pos.astype(F32)
    g = jnp.where(valid, gate, NEG)
    live_pos = jnp.where(valid, pos, big)
    sel = jnp.zeros(gate.shape, jnp.int32)
    for _ in range(k):
        m = jnp.max(g, axis=axis, keepdims=True)
        idx = jnp.min(jnp.where(g == m, live_pos, big), axis=axis, keepdims=True)
        hit = pos == idx
        sel = jnp.where(hit, 1, sel)
        live_pos = jnp.where(hit, big, live_pos)
        g = jnp.where(hit, NEG, g)
    return sel


def _moba_prompt_kernel(q_ref, k_ref, v_ref, g_ref, bias_ref, o_ref, kaug, vaug, kmat, s_scr, *, nb, group):
    qi = pl.program_id(2)
    blk = MOBA_BLOCK
    lane = lax.broadcasted_iota(jnp.int32, (blk, LANE), 1)

    @pl.when(qi == 0)
    def _():
        s = k_ref.shape[1]
        kmean = jnp.sum(k_ref[0].reshape(nb, blk, LANE), axis=1) * (1.0 / blk)
        lane_nb = lax.broadcasted_iota(jnp.int32, (nb, LANE), 1)
        for hh in range(2):
            free0 = (1 - hh) * DH_A
            kmat[hh] = jnp.zeros((LANE, LANE), F32)
            kmat[hh, free0:free0 + nb, :] = jnp.where(lane_nb // DH_A == hh, kmean, 0.0)

        def fill(j, carry):
            rows = pl.ds(pl.multiple_of(j * blk, blk), blk)
            kj = k_ref[0, rows, :]
            vj = v_ref[0, rows, :]
            for hh in range(2):
                free0 = (1 - hh) * DH_A
                own = lane // DH_A == hh
                kaug[hh, rows, :] = jnp.where(own, kj, jnp.where(lane == free0 + j, 1.0, 0.0)).astype(BF16)
                vaug[hh, rows, :] = jnp.where(own, vj, 1.0).astype(BF16)
            return carry

        lax.fori_loop(0, s // blk, fill, 0)

    q = q_ref[0] * (DH_A ** -0.5)
    blk_row = lax.broadcasted_iota(jnp.int32, (nb, blk), 0)
    sel_bias = {}
    for hh in range(2):
        free0 = (1 - hh) * DH_A
        qh = jnp.where(lane // DH_A == hh, q, 0.0)
        gate_t = _dot(kmat[hh], qh, NT)[free0:free0 + nb]
        sel = _select_topk(gate_t, blk_row < qi, blk_row, MOBA_TOPK, axis=0)
        attend = jnp.where(blk_row == qi, 1, sel)
        sel_bias[free0] = jnp.where(attend > 0, 0.0, NEG)
    gap = jnp.zeros((DH_A - nb, blk), F32)
    bias_lanes = jnp.concatenate([sel_bias[0], gap, sel_bias[DH_A], gap], axis=0).T
    q_aug = [jnp.where(lane // DH_A == hh, q, bias_lanes).astype(BF16) for hh in range(2)]

    ngroups = (qi + group) // group

    def scores(gi, mrun):
        mrun = list(mrun)
        for u in range(group):
            j = gi * group + u
            rows = pl.ds(pl.multiple_of(j * blk, blk), blk)
            for hh in range(2):
                s = _dot(q_aug[hh], kaug[hh, rows, :], NT) + bias_ref[hh, jnp.clip(qi - j, 0, 2)]
                s_scr[hh, j] = s
                mrun[hh] = jnp.maximum(mrun[hh], jnp.maximum(s[:, :LANE], s[:, LANE:]))
        return tuple(mrun)

    mrun = lax.fori_loop(0, ngroups, scores, (jnp.full((blk, LANE), M_INIT, F32),) * 2)
    m = [jnp.max(mr, axis=-1, keepdims=True) for mr in mrun]

    def weighted(gi, acc):
        acc = list(acc)
        for u in range(group):
            j = gi * group + u
            rows = pl.ds(pl.multiple_of(j * blk, blk), blk)
            for hh in range(2):
                acc[hh] = acc[hh] + _dot(jnp.exp(s_scr[hh, j] - m[hh]), vaug[hh, rows, :])
        return tuple(acc)

    acc = lax.fori_loop(0, ngroups, weighted, (jnp.zeros((blk, LANE), F32),) * 2)
    outs = [a / pltpu.roll(a, DH_A, axis=1) for a in acc]
    o_ref[0] = _silu(g_ref[0]) * jnp.where(lane < DH_A, outs[0], outs[1])


def moba_prompt(z, bias):
    bsz, s, _ = z.shape
    nh = bias.shape[0]
    npair = nh // 2
    nb = s // MOBA_BLOCK
    assert s % MOBA_BLOCK == 0 and nb <= DH_A and 2 * DH_A == LANE
    blk = MOBA_BLOCK
    group = next(g for g in (4, 2, 1) if nb % g == 0)
    return pl.pallas_call(
        functools.partial(_moba_prompt_kernel, nb=nb, group=group),
        grid=(bsz, npair, nb),
        in_specs=[pl.BlockSpec((1, blk, LANE), lambda b, p, i: (b, i, p)),
                  pl.BlockSpec((1, s, LANE), lambda b, p, i: (b, 0, npair + p)),
                  pl.BlockSpec((1, s, LANE), lambda b, p, i: (b, 0, 2 * npair + p)),
                  pl.BlockSpec((1, blk, LANE), lambda b, p, i: (b, i, 3 * npair + p)),
                  pl.BlockSpec((2, 3, blk, blk), lambda b, p, i: (p, 0, 0, 0))],
        out_specs=pl.BlockSpec((1, blk, LANE), lambda b, p, i: (b, i, p)),
        out_shape=jax.ShapeDtypeStruct((bsz, s, nh * DH_A), F32),
        scratch_shapes=[pltpu.VMEM((2, s, LANE), BF16), pltpu.VMEM((2, s, LANE), BF16),
                        pltpu.VMEM((2, LANE, LANE), F32), pltpu.VMEM((2, nb, blk, blk), F32)],
        compiler_params=_params("arbitrary", "arbitrary", "arbitrary"),
        name="moba_prompt",
    )(z, z, z, z, bias)


def _moba_sample_kernel(pt_ref, q_ref, kn_ref, vn_ref, g_ref, *refs, nb, nh, bps):
    del pt_ref
    k_refs, v_refs = refs[:2 * bps], refs[2 * bps:4 * bps]
    bias_past_ref, bias_own_ref, o_ref, qexp, acc_scr, m_all, l_all, gate_all = refs[4 * bps:]
    step = pl.program_id(1)
    t8 = SUBLANE
    rows = nh * t8
    width = nh * DH_A
    page = k_refs[0].shape[2]
    row_head = lax.broadcasted_iota(jnp.int32, (rows, width), 0) // t8
    lane_head = lax.broadcasted_iota(jnp.int32, (rows, width), 1) // DH_A
    head_mask = row_head == lane_head
    lane = lax.broadcasted_iota(jnp.int32, (rows, LANE), 1)

    @pl.when(step == 0)
    def _():
        q8 = q_ref[0] * (DH_A ** -0.5)
        qexp[...] = jnp.where(head_mask, jnp.concatenate([q8] * nh, axis=0), 0.0).astype(BF16)
        m_all[...] = jnp.zeros((rows, LANE), F32)
        l_all[...] = jnp.zeros((rows, LANE), F32)
        gate_all[...] = jnp.zeros((rows, LANE), F32)

    qe = qexp[...]
    m_new, l_new, gate_new = m_all[...], l_all[...], gate_all[...]
    for u in range(bps):
        j = step * bps + u
        kt_a, kt_b = k_refs[2 * u][0], k_refs[2 * u + 1][0]
        raw_a = _dot(qe, kt_a)
        raw_b = _dot(qe, kt_b)
        gate = jnp.sum(raw_a + raw_b, axis=-1, keepdims=True) * (1.0 / MOBA_BLOCK)
        bias = bias_past_ref[jnp.where(j == nb - 1, 0, 1)]
        s_a = raw_a + bias[:, :page]
        s_b = raw_b + bias[:, page:]
        m_j = jnp.max(jnp.maximum(s_a, s_b), axis=-1, keepdims=True)
        p_a = jnp.exp(s_a - m_j)
        p_b = jnp.exp(s_b - m_j)
        l_j = jnp.sum(p_a + p_b, axis=-1, keepdims=True)
        acc_scr[j] = _dot(p_a, v_refs[2 * u][0], NT) + _dot(p_b, v_refs[2 * u + 1][0], NT)
        m_new = jnp.where(lane == j, m_j, m_new)
        l_new = jnp.where(lane == j, l_j, l_new)
        gate_new = jnp.where(lane == j, gate, gate_new)
    m_all[...] = m_new
    l_all[...] = l_new
    gate_all[...] = gate_new

    @pl.when(step == pl.num_programs(1) - 1)
    def _():
        pad = jnp.zeros((LANE - t8, width), F32)
        s_o = _dot(qe, jnp.concatenate([kn_ref[0], pad], axis=0), NT) + bias_own_ref[...]
        m_o = jnp.max(s_o, axis=-1, keepdims=True)
        p_o = jnp.exp(s_o - m_o)
        l_o = jnp.sum(p_o, axis=-1, keepdims=True)
        acc_o = _dot(p_o, jnp.concatenate([vn_ref[0], pad], axis=0))
        sel = _select_topk(gate_all[...], lane < nb, lane, min(MOBA_TOPK, nb)) > 0
        mm = m_all[...]
        m_tot = jnp.maximum(m_o, jnp.max(jnp.where(sel, mm, M_INIT), axis=-1, keepdims=True))
        w_all = jnp.where(sel, jnp.exp(jnp.where(sel, mm, m_tot) - m_tot), 0.0)
        w_o = jnp.exp(m_o - m_tot)
        l_tot = w_o * l_o + jnp.sum(w_all * l_all[...], axis=-1, keepdims=True)
        tot = w_o * acc_o
        for jb in range(nb):
            tot = tot + w_all[:, jb:jb + 1] * acc_scr[jb]
        tot = jnp.where(head_mask, tot / l_tot, 0.0)
        out8 = tot[0:t8]
        for h in range(1, nh):
            out8 = out8 + tot[h * t8:(h + 1) * t8]
        o_ref[0] = _silu(g_ref[0]) * out8


def moba_sample(z, cache_kt, cache_vt, page0, page_table, bias_past, bias_own):
    bs, t8, _ = z.shape
    width, page = cache_kt.shape[1:]
    nh = width // DH_A
    n_pages = page_table.shape[1]
    assert t8 == SUBLANE and MOBA_BLOCK == 2 * page and (n_pages * page) % MOBA_BLOCK == 0
    nb = n_pages * page // MOBA_BLOCK
    assert nb <= LANE
    rows = nh * t8
    bps = 2 if nb % 2 == 0 else 1
    zspec = lambda c: pl.BlockSpec((1, t8, width), lambda b, j, pt: (b, 0, c))
    pspec = lambda o: pl.BlockSpec((1, width, page), lambda b, j, pt: (page0 + pt[b, 2 * bps * j + o], 0, 0))
    pages = [pspec(o) for o in range(2 * bps)]
    grid_spec = pltpu.PrefetchScalarGridSpec(
        num_scalar_prefetch=1,
        grid=(bs, nb // bps),
        in_specs=[zspec(0), zspec(1), zspec(2), zspec(3)] + pages + pages
                 + [pl.BlockSpec((2, rows, MOBA_BLOCK), lambda b, j, pt: (0, 0, 0)),
                    pl.BlockSpec((rows, LANE), lambda b, j, pt: (0, 0))],
        out_specs=pl.BlockSpec((1, t8, width), lambda b, j, pt: (b, 0, 0)),
        scratch_shapes=[pltpu.VMEM((rows, width), BF16), pltpu.VMEM((nb, rows, width), F32),
                        pltpu.VMEM((rows, LANE), F32), pltpu.VMEM((rows, LANE), F32),
                        pltpu.VMEM((rows, LANE), F32)])
    return pl.pallas_call(
        functools.partial(_moba_sample_kernel, nb=nb, nh=nh, bps=bps),
        grid_spec=grid_spec,
        out_shape=jax.ShapeDtypeStruct((bs, t8, width), F32),
        compiler_params=_params("arbitrary", "arbitrary"),
        name="moba_sample",
    )(page_table, z, z, z, z, *([cache_kt] * (2 * bps)), *([cache_vt] * (2 * bps)), bias_past, bias_own)


def _conv_b_kernel(b_ref, c_ref, h_ref, g_ref, prev_ref, w_ref, y_ref, tail_ref, buf):
    r = c_ref.shape[1]

    @pl.when(pl.program_id(1) == 0)
    def _():
        buf[0:SUBLANE, :] = prev_ref[0]

    u = c_ref[0] * h_ref[0]
    buf[SUBLANE:SUBLANE + r, :] = u
    conv = u * w_ref[CONV_B - 1:CONV_B, :]
    for jtap in range(CONV_B - 1):
        off = SUBLANE - (CONV_B - 1) + jtap
        conv = conv + buf[off:off + r, :] * w_ref[jtap:jtap + 1, :]
    y_ref[0] = _silu(g_ref[0]) * (b_ref[0] * conv)
    tail_ref[0] = u[r - SUBLANE:, :]
    buf[0:SUBLANE, :] = u[r - SUBLANE:, :]


def conv_b(z, prev8, w8, tr, col0):
    bsz, t, _ = z.shape
    width = prev8.shape[2]
    zspec = lambda c: pl.BlockSpec((1, tr, width), lambda b, i: (b, i, c))
    return pl.pallas_call(
        _conv_b_kernel,
        grid=(bsz, t // tr),
        in_specs=[zspec(col0), zspec(col0 + 1), zspec(col0 + 2), zspec(col0 + 3),
                  pl.BlockSpec((1, SUBLANE, width), lambda b, i: (b, 0, 0)),
                  pl.BlockSpec((SUBLANE, width), lambda b, i: (0, 0))],
        out_specs=[pl.BlockSpec((1, tr, width), lambda b, i: (b, i, 0)),
                   pl.BlockSpec((1, SUBLANE, width), lambda b, i: (b, 0, 0))],
        out_shape=[jax.ShapeDtypeStruct((bsz, t, width), F32),
                   jax.ShapeDtypeStruct((bsz, SUBLANE, width), F32)],
        scratch_shapes=[pltpu.VMEM((tr + SUBLANE, width), F32)],
        compiler_params=_params("arbitrary", "arbitrary"),
        name="conv_b",
    )(z, z, z, z, prev8, w8)


def _gdn_pre_kernel(xq_ref, xk_ref, xv_ref, pq_ref, pk_ref, pv_ref, w_ref, q_ref, k_ref, v_ref, buf, *, nh):
    r = xq_ref.shape[1]
    width = xq_ref.shape[2]
    first = pl.program_id(1) == 0
    for idx, (x_ref, p_ref, o_ref) in enumerate(((xq_ref, pq_ref, q_ref), (xk_ref, pk_ref, k_ref),
                                                 (xv_ref, pv_ref, v_ref))):
        @pl.when(first)
        def _(idx=idx, p_ref=p_ref):
            buf[idx, 0:SUBLANE, :] = p_ref[0]

        x = x_ref[0]
        buf[idx, SUBLANE:SUBLANE + r, :] = x
        wcol = slice(idx * width, (idx + 1) * width)
        conv = x * w_ref[CONV_C - 1:CONV_C, wcol]
        for jtap in range(CONV_C - 1):
            off = SUBLANE - (CONV_C - 1) + jtap
            conv = conv + buf[idx, off:off + r, :] * w_ref[jtap:jtap + 1, wcol]
        buf[idx, 0:SUBLANE, :] = x[r - SUBLANE:, :]
        act = _silu(conv)
        for h in range(nh):
            a = act[:, h * DK_C:(h + 1) * DK_C]
            if idx < 2:
                a = a * lax.rsqrt(jnp.sum(a * a, axis=-1, keepdims=True) + RMS_EPS)
            o_ref[0, h] = a


def gdn_pre(z, prev8, w8, tr, col0):
    bsz, t, _ = z.shape
    width = prev8.shape[2] // 3
    nh = width // DK_C
    zspec = lambda c: pl.BlockSpec((1, tr, width), lambda b, i: (b, i, c))
    pspec = lambda c: pl.BlockSpec((1, SUBLANE, width), lambda b, i: (b, 0, c))
    ospec = pl.BlockSpec((1, nh, tr, DK_C), lambda b, i: (b, 0, i, 0))
    oshape = jax.ShapeDtypeStruct((bsz, nh, t, DK_C), F32)
    return pl.pallas_call(
        functools.partial(_gdn_pre_kernel, nh=nh),
        grid=(bsz, t // tr),
        in_specs=[zspec(col0), zspec(col0 + 1), zspec(col0 + 2), pspec(0), pspec(1), pspec(2),
                  pl.BlockSpec((SUBLANE, 3 * width), lambda b, i: (0, 0))],
        out_specs=[ospec, ospec, ospec],
        out_shape=[oshape, oshape, oshape],
        scratch_shapes=[pltpu.VMEM((3, tr + SUBLANE, width), F32)],
        compiler_params=_params("arbitrary", "arbitrary"),
        name="gdn_pre",
    )(z, z, z, prev8, prev8, prev8, w8)


def _gdn_chunk_kernel(q_ref, k_ref, v_ref, ab_ref, alog_ref, dtb_ref, s0_ref, o_ref, sout_ref, s_scr,
                      *, nh, t_valid):
    c = GDN_CHUNK
    ti = pl.program_id(1)
    tc = q_ref.shape[2]

    @pl.when(ti == 0)
    def _():
        s_scr[...] = s0_ref[0]

    row = lax.broadcasted_iota(jnp.int32, (c, c), 0)
    col = lax.broadcasted_iota(jnp.int32, (c, c), 1)
    tri = row >= col
    strict = row > col
    eye = jnp.where(row == col, 1.0, 0.0)
    tril_ones = jnp.where(tri, 1.0, 0.0)
    neg_a = -jnp.exp(alog_ref[...])
    dtb = dtb_ref[...]

    def chunk(ci, carry):
        r0 = pl.multiple_of(ci * c, c)
        ab = ab_ref[0, pl.ds(r0, c), :]
        g_all = neg_a * (jnp.maximum(ab + dtb, 0.0) + jnp.log(1.0 + jnp.exp(-jnp.abs(ab + dtb))))
        beta_all = 1.0 / (1.0 + jnp.exp(-ab))
        if t_valid is not None:
            live = (ti * tc + r0 + lax.broadcasted_iota(jnp.int32, (c, LANE), 0)) < t_valid
            g_all = jnp.where(live, g_all, 0.0)
            beta_all = jnp.where(live, beta_all, 0.0)
        gc_all = _dot(tril_ones, g_all, NN, exact=True)
        gc_t = gc_all.T
        heads = range(nh)
        gcol = [gc_all[:, h:h + 1] for h in heads]
        bcol = [beta_all[:, nh + h:nh + h + 1] for h in heads]
        decay = [jnp.where(tri, jnp.exp(jnp.where(tri, gcol[h] - gc_t[h:h + 1, :], 0.0)), 0.0) for h in heads]
        q = [q_ref[0, h, pl.ds(r0, c), :] * (DK_C ** -0.5) for h in heads]
        k = [k_ref[0, h, pl.ds(r0, c), :] for h in heads]
        v = [v_ref[0, h, pl.ds(r0, c), :] for h in heads]
        kbeta = [k[h] * bcol[h] for h in heads]
        a_mat = [jnp.where(strict, _dot(kbeta[h], k[h], NT) * decay[h], 0.0) for h in heads]
        tinv = [eye - a for a in a_mat]
        power = [_split(a) for a in a_mat]
        for _ in range(int(math.log2(c)) - 1):
            power = [_split(_dot_split(p, p)) for p in power]
            tinv = [t + _dot_split(_split(t), p) for t, p in zip(tinv, power)]
        egc = [jnp.exp(g) for g in gcol]
        u = [_dot(tinv[h], v[h] * bcol[h]) for h in heads]
        w = [_dot(tinv[h], kbeta[h] * egc[h]) for h in heads]
        s_old = [s_scr[h] for h in heads]
        v_new = [u[h] - _dot(w[h], s_old[h]) for h in heads]
        attn = [_dot(q[h], k[h], NT) * decay[h] for h in heads]
        for h in heads:
            o_ref[0, h, pl.ds(r0, c), :] = _dot(q[h] * egc[h], s_old[h]) + _dot(attn[h], v_new[h])
        for h in heads:
            g_last = gc_all[c - 1:c, h:h + 1]
            s_scr[h] = s_old[h] * jnp.exp(g_last) + _dot(k[h] * jnp.exp(g_last - gcol[h]), v_new[h], TN)
        return carry

    lax.fori_loop(0, tc // c, chunk, 0)

    @pl.when(ti == pl.num_programs(1) - 1)
    def _():
        sout_ref[0] = s_scr[...]


def gdn_chunk(q, k, v, ab, ab_col, alog_lane, dtb_lane, s0, tc, t_valid):
    bsz, nh, t, _ = q.shape
    assert t % tc == 0 and tc % GDN_CHUNK == 0
    qspec = pl.BlockSpec((1, nh, tc, DK_C), lambda b, i: (b, 0, i, 0))
    sspec = pl.BlockSpec((1, nh, DK_C, DV_C), lambda b, i: (b, 0, 0, 0))
    vec = pl.BlockSpec((1, LANE), lambda b, i: (0, 0))
    return pl.pallas_call(
        functools.partial(_gdn_chunk_kernel, nh=nh, t_valid=None if t_valid == t else t_valid),
        grid=(bsz, t // tc),
        in_specs=[qspec, qspec, qspec, pl.BlockSpec((1, tc, LANE), lambda b, i: (b, i, ab_col)), vec, vec, sspec],
        out_specs=[qspec, sspec],
        out_shape=[jax.ShapeDtypeStruct((bsz, nh, t, DV_C), F32),
                   jax.ShapeDtypeStruct((bsz, nh, DK_C, DV_C), F32)],
        scratch_shapes=[pltpu.VMEM((nh, DK_C, DV_C), F32)],
        compiler_params=_params("arbitrary", "arbitrary"),
        name="gdn_chunk",
    )(q, k, v, ab, alog_lane, dtb_lane, s0)


def _gdn_post_kernel(o_ref, nw_ref, g_ref, y_ref, *, nh):
    parts = []
    for h in range(nh):
        o = o_ref[0, h]
        parts.append(o * lax.rsqrt(jnp.mean(o * o, axis=-1, keepdims=True) + RMS_EPS) * nw_ref[...])
    y_ref[0] = _silu(g_ref[0]) * jnp.concatenate(parts, axis=-1)


def gdn_post(o, norm_w, z, tr, gate_col):
    bsz, nh, t, dv = o.shape
    return pl.pallas_call(
        functools.partial(_gdn_post_kernel, nh=nh),
        grid=(bsz, t // tr),
        in_specs=[pl.BlockSpec((1, nh, tr, dv), lambda b, i: (b, 0, i, 0)),
                  pl.BlockSpec((1, dv), lambda b, i: (0, 0)),
                  pl.BlockSpec((1, tr, nh * dv), lambda b, i: (b, i, gate_col))],
        out_specs=pl.BlockSpec((1, tr, nh * dv), lambda b, i: (b, i, 0)),
        out_shape=jax.ShapeDtypeStruct((bsz, t, nh * dv), F32),
        compiler_params=_params("arbitrary", "arbitrary"),
        name="gdn_post",
    )(o, norm_w.reshape(1, dv), z)


def _cross_attn_kernel(q_ref, mk_ref, mv_ref, o_ref):
    dh = q_ref.shape[2] // H_X
    outs = []
    for h in range(H_X):
        cols = slice(h * dh, (h + 1) * dh)
        s = _dot(q_ref[0, :, cols] * (dh ** -0.5), mk_ref[0, :, cols], NT)
        p = jnp.exp(s - jnp.max(s, axis=-1, keepdims=True))
        p = p / jnp.sum(p, axis=-1, keepdims=True)
        outs.append(_dot(p, mv_ref[0, :, cols]))
    o_ref[0] = jnp.concatenate(outs, axis=-1)


def cross_attn(q, mk, mv, row0, tr):
    bsz, t, d = q.shape
    nm = mk.shape[1]
    mspec = pl.BlockSpec((1, nm, d), lambda b, i: (row0 + b, 0, 0))
    return pl.pallas_call(
        _cross_attn_kernel,
        grid=(bsz, t // tr),
        in_specs=[pl.BlockSpec((1, tr, d), lambda b, i: (b, i, 0)), mspec, mspec],
        out_specs=pl.BlockSpec((1, tr, d), lambda b, i: (b, i, 0)),
        out_shape=jax.ShapeDtypeStruct((bsz, t, d), F32),
        compiler_params=_params("arbitrary", "arbitrary"),
        name="cross_attn",
    )(q, mk, mv)


def _pad_rows(a, rows, axis=1, front=False):
    pad = [(0, 0)] * a.ndim
    pad[axis] = (rows - a.shape[axis], 0) if front else (0, rows - a.shape[axis])
    return jnp.pad(a, pad)


def _tile(t, pref):
    return pref if t % pref == 0 else t


def _layer(x, t_valid, attn_fn, conv_b_prev, conv_c_prev, s0, mk, mv, mem_row0, lw):
    bsz, t, d = x.shape
    m = bsz * t
    w_b = lw["conv_b_w"].shape[1]
    w_c = lw["conv_c_w"].shape[1] // 3
    nh_c = w_c // DV_C
    w_a = lw["w_a"]
    assert w_a == w_b == w_c and w_a % LANE == 0, "column blocks of z are addressed in units of one group width"
    tm = _tile(m, 512)
    tr = _tile(t, 256)

    z = matmul(x.reshape(m, d), lw["w_in"], tm, lw["w_in"].shape[1] // lw["n_split"]).reshape(bsz, t, -1)
    ya = attn_fn(z)
    yb, tail_b = conv_b(z, _pad_rows(conv_b_prev, SUBLANE, front=True), _pad_rows(lw["conv_b_w"], SUBLANE, axis=0),
                        tr, 4)
    qc, kc, vc = gdn_pre(z, _pad_rows(conv_c_prev, SUBLANE, front=True), _pad_rows(lw["conv_c_w"], SUBLANE, axis=0),
                         tr, 8)
    ab_col = (12 * w_a) // LANE
    if t % GDN_CHUNK == 0:
        o, s_new = gdn_chunk(qc, kc, vc, z, ab_col, lw["alog_lane"], lw["dtb_lane"], s0, _tile(t, 512), t_valid)
    else:
        tp = GDN_CHUNK
        padt = lambda a: _pad_rows(a, tp, axis=2)
        ab = _pad_rows(z[:, :, 12 * w_a:12 * w_a + LANE], tp, axis=1)
        o, s_new = gdn_chunk(padt(qc), padt(kc), padt(vc), ab, 0, lw["alog_lane"], lw["dtb_lane"], s0, tp, t_valid)
        o = o[:, :, :t]
    yc = gdn_post(o, lw["norm_w"], z, tr, 11)

    w_out = lw["w_out"]
    x2 = outproj_ln([ya.reshape(m, w_a), yb.reshape(m, w_b), yc.reshape(m, w_c)],
                    [w_out[:w_a], w_out[w_a:w_a + w_b], w_out[w_a + w_b:]],
                    x.reshape(m, d), lw["ln1_g"], lw["ln1_b"], lw["alpha"], tm)
    q = matmul(x2, lw["wq_x"], tm, d).reshape(bsz, t, d)
    ctx = cross_attn(q, mk, mv, mem_row0, tr)
    x3 = outproj_ln([ctx.reshape(m, d)], [lw["wo_x"]], x2, lw["ln2_g"], lw["ln2_b"], lw["alpha"], tm)

    nh_a = w_a // DH_A
    k_new = z[:, :t_valid, w_a:2 * w_a].reshape(bsz, t_valid, nh_a, DH_A)
    v_new = z[:, :t_valid, 2 * w_a:3 * w_a].reshape(bsz, t_valid, nh_a, DH_A)
    tail0 = t - SUBLANE
    conv_b_new = tail_b[:, t_valid - tail0 - (CONV_B - 1):t_valid - tail0]
    conv_c_new = z[:, t_valid - (CONV_C - 1):t_valid, 8 * w_a:11 * w_a]
    return x3.reshape(bsz, t, d), k_new, v_new, conv_b_new, conv_c_new, s_new


def kernel(x_prompt, x_sample, mem_prompt, cache_attn_k, cache_attn_v, cache_mem_k, cache_mem_v, state_conv_b, state_conv_c, state_delta, page_table, w_in, conv_b_w, conv_c_w, gdn_a_log, gdn_dt_bias, gdn_norm_w, w_out, ln1_g, ln1_b, rel_bias, wq_x, wk_x, wv_x, wo_x, ln2_g, ln2_b):
    depth, d, n_in = w_in.shape
    bp, seq, _ = x_prompt.shape
    bs, dec_seq, _ = x_sample.shape
    nh_a = rel_bias.shape[1]
    w_a = nh_a * DH_A
    nh_c = gdn_a_log.shape[1]
    n_mem = mem_prompt.shape[1]
    page = cache_attn_k.shape[2]
    past = page_table.shape[1] * page
    alpha = float((2 * depth) ** 0.25)
    assert n_in == 12 * w_a + 2 * nh_c and dec_seq <= SUBLANE and 2 * nh_c <= LANE
    assert seq >= CONV_C - 1 and dec_seq >= CONV_C - 1 and past % MOBA_BLOCK == 0

    nz = 12 * w_a + 2 * LANE
    n_split = 2
    lane_vec = lambda v: _pad_rows(v.reshape(1, -1).astype(F32), LANE, axis=1)

    bias_prompt = bias_table(rel_bias, (0, MOBA_BLOCK, 2 * MOBA_BLOCK), MOBA_BLOCK, MOBA_BLOCK)
    bias_prompt = bias_prompt.reshape(nh_a, 3, MOBA_BLOCK, MOBA_BLOCK)
    nbf = past // MOBA_BLOCK
    bias_past = bias_table(rel_bias, (past - (nbf - 1) * MOBA_BLOCK, past - (nbf - 2) * MOBA_BLOCK), SUBLANE, MOBA_BLOCK)
    bias_past = bias_past.reshape(nh_a, 2, SUBLANE, MOBA_BLOCK).transpose(1, 0, 2, 3).reshape(2, nh_a * SUBLANE, MOBA_BLOCK)
    bias_own = bias_table(rel_bias, (0,), SUBLANE, LANE).reshape(nh_a * SUBLANE, LANE)

    n_phys = cache_attn_k.shape[1]
    cache_kt = cache_attn_k.transpose(0, 1, 3, 4, 2).reshape(depth * n_phys, w_a, page)
    cache_vt = cache_attn_v.transpose(0, 1, 3, 4, 2).reshape(depth * n_phys, w_a, page)

    mem_k_all = cache_mem_k.reshape(depth * bs, n_mem, d)
    mem_v_all = cache_mem_v.reshape(depth * bs, n_mem, d)

    xp = x_prompt
    xs = _pad_rows(x_sample, SUBLANE)
    outs = [[] for _ in range(12)]
    for l in range(depth):
        lw = dict(
            w_in=_pad_rows(w_in[l], nz, axis=1).astype(BF16), n_split=n_split, w_a=w_a,
            conv_b_w=conv_b_w[l], conv_c_w=conv_c_w[l],
            alog_lane=lane_vec(gdn_a_log[l]), dtb_lane=lane_vec(gdn_dt_bias[l]), norm_w=gdn_norm_w[l],
            w_out=w_out[l].astype(BF16), ln1_g=ln1_g[l], ln1_b=ln1_b[l],
            wq_x=wq_x[l].astype(BF16), wo_x=wo_x[l].astype(BF16), ln2_g=ln2_g[l], ln2_b=ln2_b[l],
            alpha=alpha)
        mem2 = mem_prompt.reshape(bp * n_mem, d)
        tmem = _tile(bp * n_mem, 512)
        mk = matmul(mem2, wk_x[l].astype(BF16), tmem, d).reshape(bp, n_mem, d)
        mv = matmul(mem2, wv_x[l].astype(BF16), tmem, d).reshape(bp, n_mem, d)

        xp, k, v, cb, cc, sd = _layer(
            xp, seq, functools.partial(moba_prompt, bias=bias_prompt),
            jnp.zeros((bp, CONV_B - 1, conv_b_w.shape[2]), F32), jnp.zeros((bp, CONV_C - 1, conv_c_w.shape[2]), F32),
            jnp.zeros((bp, nh_c, DK_C, DV_C), F32), mk, mv, 0, lw)
        for lst, val in zip(outs[:2] + outs[4:6] + [outs[6], outs[8], outs[10]],
                            (k, v, mk.reshape(bp, n_mem, H_X, d // H_X), mv.reshape(bp, n_mem, H_X, d // H_X),
                             cb, cc, sd)):
            lst.append(val)

        attn_s = functools.partial(
            moba_sample, cache_kt=cache_kt, cache_vt=cache_vt, page0=l * n_phys,
            page_table=page_table, bias_past=bias_past, bias_own=bias_own)
        xs, k, v, cb, cc, sd = _layer(
            xs, dec_seq, attn_s, state_conv_b[l], state_conv_c[l], state_delta[l],
            mem_k_all, mem_v_all, l * bs, lw)
        for lst, val in zip(outs[2:4] + [outs[7], outs[9], outs[11]], (k, v, cb, cc, sd)):
            lst.append(val)

    kp, vp, ksm, vsm, mkp, mvp, cbp, cbs, ccp, ccs, sdp, sds = (jnp.stack(o) for o in outs)
    return (xp, xs[:, :dec_seq], kp, vp, ksm, vsm, mkp, mvp, cbp, cbs, ccp, ccs, sdp, sds)
```

```python
import functools
import math

import numpy as np
import jax
import jax.numpy as jnp
from jax import lax
from jax.experimental import pallas as pl
from jax.experimental.pallas import tpu as pltpu

F32 = jnp.float32
BF16 = jnp.bfloat16

LANE = 128
SUBLANE = 8
VMEM_LIMIT = 56 * 1024 * 1024

DH_A = 64
MOBA_BLOCK = 256
MOBA_TOPK = 3
N_BUCKETS = 32
MAX_EXACT = 16
MAX_DIST = 128
CONV_B = 3
CONV_C = 4
DK_C = 64
DV_C = 64
GDN_CHUNK = 64
H_X = 4
LN_EPS = 1e-5
RMS_EPS = 1e-6
NEG = -1e30
M_INIT = -3.0e38
LOG2E = math.log2(math.e)

NN = ((1,), (0,))
NT = ((1,), (1,))
TN = ((0,), (0,))


def _dot(a, b, dims=NN, exact=False):
    dn = (dims, ((), ()))
    if exact:
        return lax.dot_general(a, b, dn, precision=lax.Precision.HIGHEST, preferred_element_type=F32)
    return lax.dot_general(a.astype(BF16), b.astype(BF16), dn, preferred_element_type=F32)


def _split(x):
    hi = x.astype(BF16)
    return hi, (x - hi.astype(F32)).astype(BF16)


def _dot_split(a, b, dims=NN):
    dn = (dims, ((), ()))
    out = lax.dot_general(a[0], b[0], dn, preferred_element_type=F32)
    out = out + lax.dot_general(a[0], b[1], dn, preferred_element_type=F32)
    return out + lax.dot_general(a[1], b[0], dn, preferred_element_type=F32)


def _silu(x):
    return x * (1.0 / (1.0 + jnp.exp(-x)))


def _params(*sem):
    return pltpu.CompilerParams(dimension_semantics=sem, vmem_limit_bytes=VMEM_LIMIT)


def _bucket_thresholds():
    n = np.arange(0, 4 * MAX_DIST, dtype=np.int32)
    ratio = np.log(np.maximum(n, 1).astype(np.float32) / np.float32(MAX_EXACT)) / np.float32(math.log(MAX_DIST / MAX_EXACT))
    large = MAX_EXACT + (ratio * np.float32(N_BUCKETS - MAX_EXACT)).astype(np.int32)
    bucket = np.where(n < MAX_EXACT, n, np.minimum(large, N_BUCKETS - 1))
    return [int(np.argmax(bucket >= b)) for b in range(N_BUCKETS)]


_THR = _bucket_thresholds()


def _bias_kernel(rel_ref, o_ref, *, bases, rows):
    h = pl.program_id(0)
    cols = o_ref.shape[2]
    for i, base in enumerate(bases):
        dist = (base + lax.broadcasted_iota(jnp.int32, (rows, cols), 0)
                - lax.broadcasted_iota(jnp.int32, (rows, cols), 1))
        val = jnp.full((rows, cols), rel_ref[0, h], F32)
        for b in range(1, N_BUCKETS):
            val = jnp.where(dist >= _THR[b], rel_ref[b, h], val)
        o_ref[0, i * rows:(i + 1) * rows, :] = jnp.where(dist >= 0, val, NEG)


def bias_table(rel_bias, bases, rows, cols):
    nh = rel_bias.shape[1]
    return pl.pallas_call(
        functools.partial(_bias_kernel, bases=tuple(bases), rows=rows),
        grid=(nh,),
        in_specs=[pl.BlockSpec(memory_space=pltpu.SMEM)],
        out_specs=pl.BlockSpec((1, len(bases) * rows, cols), lambda h: (h, 0, 0)),
        out_shape=jax.ShapeDtypeStruct((nh, len(bases) * rows, cols), F32),
        compiler_params=_params("arbitrary"),
        name="bias_table",
    )(rel_bias)


def _mm_kernel(x_ref, w_ref, o_ref):
    o_ref[...] = jnp.dot(x_ref[...].astype(BF16), w_ref[...], preferred_element_type=F32)


def matmul(x, w, tm, tn):
    m, k = x.shape
    n = w.shape[1]
    return pl.pallas_call(
        _mm_kernel,
        grid=(n // tn, m // tm),
        in_specs=[pl.BlockSpec((tm, k), lambda j, i: (i, 0)),
                  pl.BlockSpec((k, tn), lambda j, i: (0, j))],
        out_specs=pl.BlockSpec((tm, tn), lambda j, i: (i, j)),
        out_shape=jax.ShapeDtypeStruct((m, n), F32),
        compiler_params=_params("arbitrary", "arbitrary"),
        name="matmul",
    )(x, w)


def _select_topk(gate, valid, pos, k, axis=-1):
    big = float(1 << 20)
    pos = pos.astype(F32)
    g = jnp.where(valid, gate, NEG)
    live_pos = jnp.where(valid, pos, big)
    sel = jnp.zeros(gate.shape, jnp.int32)
    for _ in range(k):
        m = jnp.max(g, axis=axis, keepdims=True)
        idx = jnp.min(jnp.where(g == m, live_pos, big), axis=axis, keepdims=True)
        hit = pos == idx
        sel = jnp.where(hit, 1, sel)
        live_pos = jnp.where(hit, big, live_pos)
        g = jnp.where(hit, NEG, g)
    return sel


def _moba_prompt_kernel(q_ref, k_ref, v_ref, g_ref, bias_ref, o_ref, kaug, vaug, kmat, s_scr, *, nb, group):
    qi = pl.program_id(2)
    blk = MOBA_BLOCK
    lane = lax.broadcasted_iota(jnp.int32, (blk, LANE), 1)

    @pl.when(qi == 0)
    def _():
        s = k_ref.shape[1]
        kmean = jnp.sum(k_ref[0].reshape(nb, blk, LANE), axis=1) * (1.0 / blk)
        lane_nb = lax.broadcasted_iota(jnp.int32, (nb, LANE), 1)
        for hh in range(2):
            free0 = (1 - hh) * DH_A
            kmat[hh] = jnp.zeros((LANE, LANE), F32)
            kmat[hh, free0:free0 + nb, :] = jnp.where(lane_nb // DH_A == hh, kmean, 0.0)

        def fill(j, carry):
            rows = pl.ds(pl.multiple_of(j * blk, blk), blk)
            kj = k_ref[0, rows, :]
            vj = v_ref[0, rows, :]
            for hh in range(2):
                free0 = (1 - hh) * DH_A
                own = lane // DH_A == hh
                kaug[hh, rows, :] = jnp.where(own, kj, jnp.where(lane == free0 + j, 1.0, 0.0)).astype(BF16)
                vaug[hh, rows, :] = jnp.where(own, vj, 1.0).astype(BF16)
            return carry

        lax.fori_loop(0, s // blk, fill, 0)

    q = q_ref[0] * (DH_A ** -0.5)
    blk_row = lax.broadcasted_iota(jnp.int32, (nb, blk), 0)
    sel_bias = {}
    for hh in range(2):
        free0 = (1 - hh) * DH_A
        qh = jnp.where(lane // DH_A == hh, q, 0.0)
        gate_t = _dot(kmat[hh], qh, NT)[free0:free0 + nb]
        sel = _select_topk(gate_t, blk_row < qi, blk_row, MOBA_TOPK, axis=0)
        attend = jnp.where(blk_row == qi, 1, sel)
        sel_bias[free0] = jnp.where(attend > 0, 0.0, NEG)
    gap = jnp.zeros((DH_A - nb, blk), F32)
    bias_lanes = jnp.concatenate([sel_bias[0], gap, sel_bias[DH_A], gap], axis=0).T
    q_aug = [jnp.where(lane // DH_A == hh, q, bias_lanes).astype(BF16) for hh in range(2)]

    ngroups = (qi + group) // group

    def scores(gi, mrun):
        mrun = list(mrun)
        for u in range(group):
            j = gi * group + u
            rows = pl.ds(pl.multiple_of(j * blk, blk), blk)
            for hh in range(2):
                s = _dot(q_aug[hh], kaug[hh, rows, :], NT) + bias_ref[hh, jnp.clip(qi - j, 0, 2)]
                s = s * LOG2E
                s_scr[hh, j] = s
                mrun[hh] = jnp.maximum(mrun[hh], jnp.maximum(s[:, :LANE], s[:, LANE:]))
        return tuple(mrun)

    mrun = lax.fori_loop(0, ngroups, scores, (jnp.full((blk, LANE), M_INIT, F32),) * 2)
    m = [jnp.max(mr, axis=-1, keepdims=True) for mr in mrun]

    def weighted(gi, acc):
        acc = list(acc)
        for u in range(group):
            j = gi * group + u
            rows = pl.ds(pl.multiple_of(j * blk, blk), blk)
            for hh in range(2):
                acc[hh] = acc[hh] + _dot(jnp.exp2(s_scr[hh, j] - m[hh]), vaug[hh, rows, :])
        return tuple(acc)

    acc = lax.fori_loop(0, ngroups, weighted, (jnp.zeros((blk, LANE), F32),) * 2)
    outs = [a / pltpu.roll(a, DH_A, axis=1) for a in acc]
    o_ref[0] = _silu(g_ref[0]) * jnp.where(lane < DH_A, outs[0], outs[1])


def moba_prompt(z, bias):
    bsz, s, _ = z.shape
    nh = bias.shape[0]
    npair = nh // 2
    nb = s // MOBA_BLOCK
    assert s % MOBA_BLOCK == 0 and nb <= DH_A and 2 * DH_A == LANE
    blk = MOBA_BLOCK
    group = next(g for g in (8, 4, 2, 1) if nb % g == 0)
    return pl.pallas_call(
        functools.partial(_moba_prompt_kernel, nb=nb, group=group),
        grid=(bsz, npair, nb),
        in_specs=[pl.BlockSpec((1, blk, LANE), lambda b, p, i: (b, i, p)),
                  pl.BlockSpec((1, s, LANE), lambda b, p, i: (b, 0, npair + p)),
                  pl.BlockSpec((1, s, LANE), lambda b, p, i: (b, 0, 2 * npair + p)),
                  pl.BlockSpec((1, blk, LANE), lambda b, p, i: (b, i, 3 * npair + p)),
                  pl.BlockSpec((2, 3, blk, blk), lambda b, p, i: (p, 0, 0, 0))],
        out_specs=pl.BlockSpec((1, blk, LANE), lambda b, p, i: (b, i, p)),
        out_shape=jax.ShapeDtypeStruct((bsz, s, nh * DH_A), F32),
        scratch_shapes=[pltpu.VMEM((2, s, LANE), BF16), pltpu.VMEM((2, s, LANE), BF16),
                        pltpu.VMEM((2, LANE, LANE), F32), pltpu.VMEM((2, nb, blk, blk), F32)],
        compiler_params=_params("arbitrary", "arbitrary", "arbitrary"),
        name="moba_prompt",
    )(z, z, z, z, bias)


def _moba_sample_kernel(pt_ref, q_ref, kn_ref, vn_ref, g_ref, *refs, nb, nh, bps):
    del pt_ref
    k_refs, v_refs = refs[:2 * bps], refs[2 * bps:4 * bps]
    bias_past_ref, bias_own_ref, o_ref, qexp, acc_scr, m_all, l_all, gate_all = refs[4 * bps:]
    step = pl.program_id(1)
    t8 = SUBLANE
    rows = nh * t8
    width = nh * DH_A
    page = k_refs[0].shape[2]
    row_head = lax.broadcasted_iota(jnp.int32, (rows, width), 0) // t8
    lane_head = lax.broadcasted_iota(jnp.int32, (rows, width), 1) // DH_A
    head_mask = row_head == lane_head
    lane = lax.broadcasted_iota(jnp.int32, (rows, LANE), 1)

    @pl.when(step == 0)
    def _():
        q8 = q_ref[0] * (DH_A ** -0.5)
        qexp[...] = jnp.where(head_mask, jnp.concatenate([q8] * nh, axis=0), 0.0).astype(BF16)
        m_all[...] = jnp.zeros((rows, LANE), F32)
        l_all[...] = jnp.zeros((rows, LANE), F32)
        gate_all[...] = jnp.zeros((rows, LANE), F32)

    qe = qexp[...]
    m_new, l_new, gate_new = m_all[...], l_all[...], gate_all[...]
    blocks = [step * bps + u for u in range(bps)]
    raw = [(_dot(qe, k_refs[2 * u][0]), _dot(qe, k_refs[2 * u + 1][0])) for u in range(bps)]
    gates = [jnp.sum(ra + rb, axis=-1, keepdims=True) * (1.0 / MOBA_BLOCK) for ra, rb in raw]
    bias = [bias_past_ref[jnp.where(j == nb - 1, 0, 1)] for j in blocks]
    s = [(ra + b[:, :page], rb + b[:, page:]) for (ra, rb), b in zip(raw, bias)]
    m_blk = [jnp.max(jnp.maximum(sa, sb), axis=-1, keepdims=True) for sa, sb in s]
    p = [(jnp.exp(sa - m), jnp.exp(sb - m)) for (sa, sb), m in zip(s, m_blk)]
    l_blk = [jnp.sum(pa + pb, axis=-1, keepdims=True) for pa, pb in p]
    pv = [_dot(pa, v_refs[2 * u][0], NT) + _dot(pb, v_refs[2 * u + 1][0], NT) for u, (pa, pb) in enumerate(p)]
    for u, j in enumerate(blocks):
        acc_scr[j] = pv[u]
        m_new = jnp.where(lane == j, m_blk[u], m_new)
        l_new = jnp.where(lane == j, l_blk[u], l_new)
        gate_new = jnp.where(lane == j, gates[u], gate_new)
    m_all[...] = m_new
    l_all[...] = l_new
    gate_all[...] = gate_new

    @pl.when(step == pl.num_programs(1) - 1)
    def _():
        pad = jnp.zeros((LANE - t8, width), F32)
        s_o = _dot(qe, jnp.concatenate([kn_ref[0], pad], axis=0), NT) + bias_own_ref[...]
        m_o = jnp.max(s_o, axis=-1, keepdims=True)
        p_o = jnp.exp(s_o - m_o)
        l_o = jnp.sum(p_o, axis=-1, keepdims=True)
        acc_o = _dot(p_o, jnp.concatenate([vn_ref[0], pad], axis=0))
        sel = _select_topk(gate_all[...], lane < nb, lane, min(MOBA_TOPK, nb)) > 0
        mm = m_all[...]
        m_tot = jnp.maximum(m_o, jnp.max(jnp.where(sel, mm, M_INIT), axis=-1, keepdims=True))
        w_all = jnp.where(sel, jnp.exp(jnp.where(sel, mm, m_tot) - m_tot), 0.0)
        w_o = jnp.exp(m_o - m_tot)
        l_tot = w_o * l_o + jnp.sum(w_all * l_all[...], axis=-1, keepdims=True)
        tot = w_o * acc_o
        for jb in range(nb):
            tot = tot + w_all[:, jb:jb + 1] * acc_scr[jb]
        tot = jnp.where(head_mask, tot / l_tot, 0.0)
        out8 = tot[0:t8]
        for h in range(1, nh):
            out8 = out8 + tot[h * t8:(h + 1) * t8]
        o_ref[0] = _silu(g_ref[0]) * out8


def moba_sample(z, cache_kt, cache_vt, page0, page_table, bias_past, bias_own):
    bs, t8, _ = z.shape
    width, page = cache_kt.shape[1:]
    nh = width // DH_A
    n_pages = page_table.shape[1]
    assert t8 == SUBLANE and MOBA_BLOCK == 2 * page and (n_pages * page) % MOBA_BLOCK == 0
    nb = n_pages * page // MOBA_BLOCK
    assert nb <= LANE
    rows = nh * t8
    bps = next(g for g in (4, 2, 1) if nb % g == 0)
    zspec = lambda c: pl.BlockSpec((1, t8, width), lambda b, j, pt: (b, 0, c))
    pspec = lambda o: pl.BlockSpec((1, width, page), lambda b, j, pt: (page0 + pt[b, 2 * bps * j + o], 0, 0))
    pages = [pspec(o) for o in range(2 * bps)]
    grid_spec = pltpu.PrefetchScalarGridSpec(
        num_scalar_prefetch=1,
        grid=(bs, nb // bps),
        in_specs=[zspec(0), zspec(1), zspec(2), zspec(3)] + pages + pages
                 + [pl.BlockSpec((2, rows, MOBA_BLOCK), lambda b, j, pt: (0, 0, 0)),
                    pl.BlockSpec((rows, LANE), lambda b, j, pt: (0, 0))],
        out_specs=pl.BlockSpec((1, t8, width), lambda b, j, pt: (b, 0, 0)),
        scratch_shapes=[pltpu.VMEM((rows, width), BF16), pltpu.VMEM((nb, rows, width), F32),
                        pltpu.VMEM((rows, LANE), F32), pltpu.VMEM((rows, LANE), F32),
                        pltpu.VMEM((rows, LANE), F32)])
    return pl.pallas_call(
        functools.partial(_moba_sample_kernel, nb=nb, nh=nh, bps=bps),
        grid_spec=grid_spec,
        out_shape=jax.ShapeDtypeStruct((bs, t8, width), F32),
        compiler_params=_params("arbitrary", "arbitrary"),
        name="moba_sample",
    )(page_table, z, z, z, z, *([cache_kt] * (2 * bps)), *([cache_vt] * (2 * bps)), bias_past, bias_own)


def _conv_b_kernel(b_ref, c_ref, h_ref, g_ref, prev_ref, w_ref, y_ref, tail_ref, buf):
    r = c_ref.shape[1]

    @pl.when(pl.program_id(1) == 0)
    def _():
        buf[0:SUBLANE, :] = prev_ref[0]

    u = c_ref[0] * h_ref[0]
    buf[SUBLANE:SUBLANE + r, :] = u
    conv = u * w_ref[CONV_B - 1:CONV_B, :]
    for jtap in range(CONV_B - 1):
        off = SUBLANE - (CONV_B - 1) + jtap
        conv = conv + buf[off:off + r, :] * w_ref[jtap:jtap + 1, :]
    y_ref[0] = _silu(g_ref[0]) * (b_ref[0] * conv)
    tail_ref[0] = u[r - SUBLANE:, :]
    buf[0:SUBLANE, :] = u[r - SUBLANE:, :]


def conv_b(z, prev8, w8, tr, col0):
    bsz, t, _ = z.shape
    width = prev8.shape[2]
    zspec = lambda c: pl.BlockSpec((1, tr, width), lambda b, i: (b, i, c))
    return pl.pallas_call(
        _conv_b_kernel,
        grid=(bsz, t // tr),
        in_specs=[zspec(col0), zspec(col0 + 1), zspec(col0 + 2), zspec(col0 + 3),
                  pl.BlockSpec((1, SUBLANE, width), lambda b, i: (b, 0, 0)),
                  pl.BlockSpec((SUBLANE, width), lambda b, i: (0, 0))],
        out_specs=[pl.BlockSpec((1, tr, width), lambda b, i: (b, i, 0)),
                   pl.BlockSpec((1, SUBLANE, width), lambda b, i: (b, 0, 0))],
        out_shape=[jax.ShapeDtypeStruct((bsz, t, width), F32),
                   jax.ShapeDtypeStruct((bsz, SUBLANE, width), F32)],
        scratch_shapes=[pltpu.VMEM((tr + SUBLANE, width), F32)],
        compiler_params=_params("arbitrary", "arbitrary"),
        name="conv_b",
    )(z, z, z, z, prev8, w8)


def _head_sums(x, ones_ref):
    hi, lo = _split(x)
    ones = ones_ref[...]
    return jnp.dot(hi, ones, preferred_element_type=F32) + jnp.dot(lo, ones, preferred_element_type=F32)


def _gdn_pre_kernel(xq_ref, xk_ref, xv_ref, pq_ref, pk_ref, pv_ref, w_ref, ones_ref, q_ref, k_ref, v_ref, buf):
    r = xq_ref.shape[1]
    width = xq_ref.shape[2]
    first = pl.program_id(1) == 0
    for idx, (x_ref, p_ref, o_ref) in enumerate(((xq_ref, pq_ref, q_ref), (xk_ref, pk_ref, k_ref),
                                                 (xv_ref, pv_ref, v_ref))):
        @pl.when(first)
        def _(idx=idx, p_ref=p_ref):
            buf[idx, 0:SUBLANE, :] = p_ref[0]

        x = x_ref[0]
        buf[idx, SUBLANE:SUBLANE + r, :] = x
        wcol = slice(idx * width, (idx + 1) * width)
        conv = x * w_ref[CONV_C - 1:CONV_C, wcol]
        for jtap in range(CONV_C - 1):
            off = SUBLANE - (CONV_C - 1) + jtap
            conv = conv + buf[idx, off:off + r, :] * w_ref[jtap:jtap + 1, wcol]
        buf[idx, 0:SUBLANE, :] = x[r - SUBLANE:, :]
        act = _silu(conv)
        if idx < 2:
            act = act * lax.rsqrt(_head_sums(act * act, ones_ref) + RMS_EPS)
        o_ref[0] = act


def _head_ones(width, head):
    ids = np.arange(width) // head
    return jnp.asarray(ids[:, None] == ids[None, :], BF16)


def gdn_pre(z, prev8, w8, tr, col0):
    bsz, t, _ = z.shape
    width = prev8.shape[2] // 3
    zspec = lambda c: pl.BlockSpec((1, tr, width), lambda b, i: (b, i, c))
    pspec = lambda c: pl.BlockSpec((1, SUBLANE, width), lambda b, i: (b, 0, c))
    ospec = pl.BlockSpec((1, tr, width), lambda b, i: (b, i, 0))
    oshape = jax.ShapeDtypeStruct((bsz, t, width), F32)
    return pl.pallas_call(
        _gdn_pre_kernel,
        grid=(bsz, t // tr),
        in_specs=[zspec(col0), zspec(col0 + 1), zspec(col0 + 2), pspec(0), pspec(1), pspec(2),
                  pl.BlockSpec((SUBLANE, 3 * width), lambda b, i: (0, 0)),
                  pl.BlockSpec((width, width), lambda b, i: (0, 0))],
        out_specs=[ospec, ospec, ospec],
        out_shape=[oshape, oshape, oshape],
        scratch_shapes=[pltpu.VMEM((3, tr + SUBLANE, width), F32)],
        compiler_params=_params("arbitrary", "arbitrary"),
        name="gdn_pre",
    )(z, z, z, prev8, prev8, prev8, w8, _head_ones(width, DK_C))


def _gdn_chunk_kernel(q_ref, k_ref, v_ref, ab_ref, alog_ref, dtb_ref, s0_ref, o_ref, sout_ref, s_scr,
                      *, nh, t_valid):
    c = GDN_CHUNK
    ti = pl.program_id(1)
    tc = q_ref.shape[1]

    @pl.when(ti == 0)
    def _():
        s_scr[...] = s0_ref[0]

    row = lax.broadcasted_iota(jnp.int32, (c, c), 0)
    col = lax.broadcasted_iota(jnp.int32, (c, c), 1)
    tri = row >= col
    strict = row > col
    eye = jnp.where(row == col, 1.0, 0.0)
    tril_ones = jnp.where(tri, 1.0, 0.0)
    neg_a = -jnp.exp(alog_ref[...])
    dtb = dtb_ref[...]

    def chunk(ci, carry):
        r0 = pl.multiple_of(ci * c, c)
        ab = ab_ref[0, pl.ds(r0, c), :]
        g_all = neg_a * (jnp.maximum(ab + dtb, 0.0) + jnp.log(1.0 + jnp.exp(-jnp.abs(ab + dtb))))
        beta_all = 1.0 / (1.0 + jnp.exp(-ab))
        if t_valid is not None:
            live = (ti * tc + r0 + lax.broadcasted_iota(jnp.int32, (c, LANE), 0)) < t_valid
            g_all = jnp.where(live, g_all, 0.0)
            beta_all = jnp.where(live, beta_all, 0.0)
        gc_all = _dot(tril_ones, g_all, NN, exact=True)
        gc_t = gc_all.T
        heads = range(nh)
        gcol = [gc_all[:, h:h + 1] for h in heads]
        bcol = [beta_all[:, nh + h:nh + h + 1] for h in heads]
        decay = [jnp.where(tri, jnp.exp(jnp.where(tri, gcol[h] - gc_t[h:h + 1, :], 0.0)), 0.0) for h in heads]
        head_cols = lambda x, h: x[:, h * DK_C:(h + 1) * DK_C]
        q_all = q_ref[0, pl.ds(r0, c), :] * (DK_C ** -0.5)
        k_all = k_ref[0, pl.ds(r0, c), :]
        v_all = v_ref[0, pl.ds(r0, c), :]
        q = [head_cols(q_all, h) for h in heads]
        k = [head_cols(k_all, h) for h in heads]
        v = [head_cols(v_all, h) for h in heads]
        kbeta = [k[h] * bcol[h] for h in heads]
        a_mat = [jnp.where(strict, _dot(kbeta[h], k[h], NT) * decay[h], 0.0) for h in heads]
        tinv = [eye - a for a in a_mat]
        power = [_split(a) for a in a_mat]
        for _ in range(int(math.log2(c)) - 1):
            power = [_split(_dot_split(p, p)) for p in power]
            tinv = [t + _dot_split(_split(t), p) for t, p in zip(tinv, power)]
        egc = [jnp.exp(g) for g in gcol]
        u = [_dot(tinv[h], v[h] * bcol[h]) for h in heads]
        w = [_dot(tinv[h], kbeta[h] * egc[h]) for h in heads]
        s_old = [s_scr[h] for h in heads]
        v_new = [u[h] - _dot(w[h], s_old[h]) for h in heads]
        attn = [_dot(q[h], k[h], NT) * decay[h] for h in heads]
        o = [_dot(q[h] * egc[h], s_old[h]) + _dot(attn[h], v_new[h]) for h in heads]
        o_ref[0, pl.ds(r0, c), :] = jnp.concatenate(o, axis=-1)
        for h in heads:
            g_last = gc_all[c - 1:c, h:h + 1]
            s_scr[h] = s_old[h] * jnp.exp(g_last) + _dot(k[h] * jnp.exp(g_last - gcol[h]), v_new[h], TN)
        return carry

    lax.fori_loop(0, tc // c, chunk, 0)

    @pl.when(ti == pl.num_programs(1) - 1)
    def _():
        sout_ref[0] = s_scr[...]


def gdn_chunk(q, k, v, ab, ab_col, alog_lane, dtb_lane, s0, tc, t_valid):
    bsz, t, width = q.shape
    nh = width // DK_C
    assert t % tc == 0 and tc % GDN_CHUNK == 0 and DK_C == DV_C
    qspec = pl.BlockSpec((1, tc, width), lambda b, i: (b, i, 0))
    sspec = pl.BlockSpec((1, nh, DK_C, DV_C), lambda b, i: (b, 0, 0, 0))
    vec = pl.BlockSpec((1, LANE), lambda b, i: (0, 0))
    return pl.pallas_call(
        functools.partial(_gdn_chunk_kernel, nh=nh, t_valid=None if t_valid == t else t_valid),
        grid=(bsz, t // tc),
        in_specs=[qspec, qspec, qspec, pl.BlockSpec((1, tc, LANE), lambda b, i: (b, i, ab_col)), vec, vec, sspec],
        out_specs=[qspec, sspec],
        out_shape=[jax.ShapeDtypeStruct((bsz, t, width), F32),
                   jax.ShapeDtypeStruct((bsz, nh, DK_C, DV_C), F32)],
        scratch_shapes=[pltpu.VMEM((nh, DK_C, DV_C), F32)],
        compiler_params=_params("arbitrary", "arbitrary"),
        name="gdn_chunk",
    )(q, k, v, ab, alog_lane, dtb_lane, s0)


def _layer_norm(hid, g_ref, b_ref):
    mu = jnp.mean(hid, axis=-1, keepdims=True)
    cen = hid - mu
    var = jnp.mean(cen * cen, axis=-1, keepdims=True)
    return cen * lax.rsqrt(var + LN_EPS) * g_ref[...] + b_ref[...]


def _layer_tail_kernel(ya_ref, yb_ref, oc_ref, gc_ref, nw_ref, ones_ref, wa_ref, wb_ref, wc_ref, x_ref, g1_ref, b1_ref,
                       wq_ref, mk_ref, mv_ref, wo_ref, g2_ref, b2_ref, o_ref, *, alpha):
    oc = oc_ref[0]
    yc = _silu(gc_ref[0]) * (oc * lax.rsqrt(_head_sums(oc * oc, ones_ref) * (1.0 / DV_C) + RMS_EPS) * nw_ref[...])
    y = _dot(ya_ref[0], wa_ref[...]) + _dot(yb_ref[0], wb_ref[...]) + _dot(yc, wc_ref[...])
    x1 = _layer_norm(alpha * x_ref[0] + y, g1_ref, b1_ref)
    q = _dot(x1, wq_ref[...])
    dh = q.shape[1] // H_X
    ctx = []
    for h in range(H_X):
        cols = slice(h * dh, (h + 1) * dh)
        s = _dot(q[:, cols] * (dh ** -0.5), mk_ref[0, :, cols], NT)
        p = jnp.exp(s - jnp.max(s, axis=-1, keepdims=True))
        p = p / jnp.sum(p, axis=-1, keepdims=True)
        ctx.append(_dot(p, mv_ref[0, :, cols]))
    o_ref[0] = _layer_norm(alpha * x1 + _dot(jnp.concatenate(ctx, axis=-1), wo_ref[...]), g2_ref, b2_ref)


def layer_tail(ya, yb, oc, z, gate_col, norm_w, w_out, x, ln1, wq, mk, mv, row0, wo, ln2, alpha, tr):
    bsz, t, d = x.shape
    nm = mk.shape[1]
    widths = (ya.shape[2], yb.shape[2], oc.shape[2])
    offs = (0, widths[0], widths[0] + widths[1])
    row = lambda w: pl.BlockSpec((1, tr, w), lambda b, i: (b, i, 0))
    full = lambda a: pl.BlockSpec(a.shape, lambda b, i: (0,) * a.ndim)
    mspec = pl.BlockSpec((1, nm, d), lambda b, i: (row0 + b, 0, 0))
    w_parts = [w_out[o:o + w] for o, w in zip(offs, widths)]
    vecs = [v.reshape(1, d) for v in (*ln1, *ln2)]
    nw = jnp.tile(norm_w, widths[2] // DV_C).reshape(1, widths[2])
    ones = _head_ones(widths[2], DV_C)
    return pl.pallas_call(
        functools.partial(_layer_tail_kernel, alpha=alpha),
        grid=(bsz, t // tr),
        in_specs=[row(widths[0]), row(widths[1]), row(widths[2]),
                  pl.BlockSpec((1, tr, widths[2]), lambda b, i: (b, i, gate_col)), full(nw), full(ones)]
                 + [full(w) for w in w_parts]
                 + [row(d), full(vecs[0]), full(vecs[1]), full(wq), mspec, mspec, full(wo), full(vecs[2]), full(vecs[3])],
        out_specs=row(d),
        out_shape=jax.ShapeDtypeStruct((bsz, t, d), F32),
        compiler_params=_params("arbitrary", "arbitrary"),
        name="layer_tail",
    )(ya, yb, oc, z, nw, ones, *w_parts, x, vecs[0], vecs[1], wq, mk, mv, wo, vecs[2], vecs[3])


def _pad_rows(a, rows, axis=1, front=False):
    pad = [(0, 0)] * a.ndim
    pad[axis] = (rows - a.shape[axis], 0) if front else (0, rows - a.shape[axis])
    return jnp.pad(a, pad)


def _tile(t, pref):
    return pref if t % pref == 0 else t


def _layer(x, t_valid, attn_fn, conv_b_prev, conv_c_prev, s0, mk, mv, mem_row0, lw):
    bsz, t, d = x.shape
    m = bsz * t
    w_b = lw["conv_b_w"].shape[1]
    w_c = lw["conv_c_w"].shape[1] // 3
    nh_c = w_c // DV_C
    w_a = lw["w_a"]
    assert w_a == w_b == w_c and w_a % LANE == 0, "column blocks of z are addressed in units of one group width"
    tm = _tile(m, 512)
    tr = _tile(t, 256)

    z = matmul(x.reshape(m, d), lw["w_in"], tm, lw["w_in"].shape[1] // lw["n_split"]).reshape(bsz, t, -1)
    ya = attn_fn(z)
    yb, tail_b = conv_b(z, _pad_rows(conv_b_prev, SUBLANE, front=True), _pad_rows(lw["conv_b_w"], SUBLANE, axis=0),
                        tr, 4)
    qc, kc, vc = gdn_pre(z, _pad_rows(conv_c_prev, SUBLANE, front=True), _pad_rows(lw["conv_c_w"], SUBLANE, axis=0),
                         tr, 8)
    ab_col = (12 * w_a) // LANE
    if t % GDN_CHUNK == 0:
        o, s_new = gdn_chunk(qc, kc, vc, z, ab_col, lw["alog_lane"], lw["dtb_lane"], s0, _tile(t, 512), t_valid)
    else:
        tp = GDN_CHUNK
        padt = lambda a: _pad_rows(a, tp, axis=1)
        ab = padt(z[:, :, 12 * w_a:12 * w_a + LANE])
        o, s_new = gdn_chunk(padt(qc), padt(kc), padt(vc), ab, 0, lw["alog_lane"], lw["dtb_lane"], s0, tp, t_valid)
        o = o[:, :t]

    x3 = layer_tail(ya, yb, o, z, 11, lw["norm_w"], lw["w_out"], x, (lw["ln1_g"], lw["ln1_b"]), lw["wq_x"], mk, mv,
                    mem_row0, lw["wo_x"], (lw["ln2_g"], lw["ln2_b"]), lw["alpha"], tr)

    nh_a = w_a // DH_A
    k_new = z[:, :t_valid, w_a:2 * w_a].reshape(bsz, t_valid, nh_a, DH_A)
    v_new = z[:, :t_valid, 2 * w_a:3 * w_a].reshape(bsz, t_valid, nh_a, DH_A)
    tail0 = t - SUBLANE
    conv_b_new = tail_b[:, t_valid - tail0 - (CONV_B - 1):t_valid - tail0]
    conv_c_new = z[:, t_valid - (CONV_C - 1):t_valid, 8 * w_a:11 * w_a]
    return x3, k_new, v_new, conv_b_new, conv_c_new, s_new


def kernel(x_prompt, x_sample, mem_prompt, cache_attn_k, cache_attn_v, cache_mem_k, cache_mem_v, state_conv_b, state_conv_c, state_delta, page_table, w_in, conv_b_w, conv_c_w, gdn_a_log, gdn_dt_bias, gdn_norm_w, w_out, ln1_g, ln1_b, rel_bias, wq_x, wk_x, wv_x, wo_x, ln2_g, ln2_b):
    depth, d, n_in = w_in.shape
    bp, seq, _ = x_prompt.shape
    bs, dec_seq, _ = x_sample.shape
    nh_a = rel_bias.shape[1]
    w_a = nh_a * DH_A
    nh_c = gdn_a_log.shape[1]
    n_mem = mem_prompt.shape[1]
    page = cache_attn_k.shape[2]
    past = page_table.shape[1] * page
    alpha = float((2 * depth) ** 0.25)
    assert n_in == 12 * w_a + 2 * nh_c and dec_seq <= SUBLANE and 2 * nh_c <= LANE
    assert seq >= CONV_C - 1 and dec_seq >= CONV_C - 1 and past % MOBA_BLOCK == 0

    nz = 12 * w_a + 2 * LANE
    n_split = 2
    lane_vec = lambda v: _pad_rows(v.reshape(1, -1).astype(F32), LANE, axis=1)

    bias_prompt = bias_table(rel_bias, (0, MOBA_BLOCK, 2 * MOBA_BLOCK), MOBA_BLOCK, MOBA_BLOCK)
    bias_prompt = bias_prompt.reshape(nh_a, 3, MOBA_BLOCK, MOBA_BLOCK)
    nbf = past // MOBA_BLOCK
    bias_past = bias_table(rel_bias, (past - (nbf - 1) * MOBA_BLOCK, past - (nbf - 2) * MOBA_BLOCK), SUBLANE, MOBA_BLOCK)
    bias_past = bias_past.reshape(nh_a, 2, SUBLANE, MOBA_BLOCK).transpose(1, 0, 2, 3).reshape(2, nh_a * SUBLANE, MOBA_BLOCK)
    bias_own = bias_table(rel_bias, (0,), SUBLANE, LANE).reshape(nh_a * SUBLANE, LANE)

    n_phys = cache_attn_k.shape[1]
    cache_kt = cache_attn_k.transpose(0, 1, 3, 4, 2).reshape(depth * n_phys, w_a, page)
    cache_vt = cache_attn_v.transpose(0, 1, 3, 4, 2).reshape(depth * n_phys, w_a, page)

    mem_k_all = cache_mem_k.reshape(depth * bs, n_mem, d)
    mem_v_all = cache_mem_v.reshape(depth * bs, n_mem, d)

    xp = x_prompt
    xs = _pad_rows(x_sample, SUBLANE)
    outs = [[] for _ in range(12)]
    for l in range(depth):
        lw = dict(
            w_in=_pad_rows(w_in[l], nz, axis=1).astype(BF16), n_split=n_split, w_a=w_a,
            conv_b_w=conv_b_w[l], conv_c_w=conv_c_w[l],
            alog_lane=lane_vec(gdn_a_log[l]), dtb_lane=lane_vec(gdn_dt_bias[l]), norm_w=gdn_norm_w[l],
            w_out=w_out[l].astype(BF16), ln1_g=ln1_g[l], ln1_b=ln1_b[l],
            wq_x=wq_x[l].astype(BF16), wo_x=wo_x[l].astype(BF16), ln2_g=ln2_g[l], ln2_b=ln2_b[l],
            alpha=alpha)
        mem2 = mem_prompt.reshape(bp * n_mem, d)
        tmem = _tile(bp * n_mem, 512)
        mk = matmul(mem2, wk_x[l].astype(BF16), tmem, d).reshape(bp, n_mem, d)
        mv = matmul(mem2, wv_x[l].astype(BF16), tmem, d).reshape(bp, n_mem, d)

        xp, k, v, cb, cc, sd = _layer(
            xp, seq, functools.partial(moba_prompt, bias=bias_prompt),
            jnp.zeros((bp, CONV_B - 1, conv_b_w.shape[2]), F32), jnp.zeros((bp, CONV_C - 1, conv_c_w.shape[2]), F32),
            jnp.zeros((bp, nh_c, DK_C, DV_C), F32), mk, mv, 0, lw)
        for lst, val in zip(outs[:2] + outs[4:6] + [outs[6], outs[8], outs[10]],
                            (k, v, mk.reshape(bp, n_mem, H_X, d // H_X), mv.reshape(bp, n_mem, H_X, d // H_X),
                             cb, cc, sd)):
            lst.append(val)

        attn_s = functools.partial(
            moba_sample, cache_kt=cache_kt, cache_vt=cache_vt, page0=l * n_phys,
            page_table=page_table, bias_past=bias_past, bias_own=bias_own)
        xs, k, v, cb, cc, sd = _layer(
            xs, dec_seq, attn_s, state_conv_b[l], state_conv_c[l], state_delta[l],
            mem_k_all, mem_v_all, l * bs, lw)
        for lst, val in zip(outs[2:4] + [outs[7], outs[9], outs[11]], (k, v, cb, cc, sd)):
            lst.append(val)

    kp, vp, ksm, vsm, mkp, mvp, cbp, cbs, ccp, ccs, sdp, sds = (jnp.stack(o) for o in outs)
    return (xp, xs[:, :dec_seq], kp, vp, ksm, vsm, mkp, mvp, cbp, cbs, ccp, ccs, sdp, sds)
```

```python
import functools
import math

import numpy as np
import jax
import jax.numpy as jnp
from jax import lax
from jax.experimental import pallas as pl
from jax.experimental.pallas import tpu as pltpu

F32 = jnp.float32
BF16 = jnp.bfloat16

LANE = 128
SUBLANE = 8
VMEM_LIMIT = 56 * 1024 * 1024

DH_A = 64
MOBA_BLOCK = 256
MOBA_TOPK = 3
N_BUCKETS = 32
MAX_EXACT = 16
MAX_DIST = 128
CONV_B = 3
CONV_C = 4
DK_C = 64
DV_C = 64
GDN_CHUNK = 64
H_X = 4
LN_EPS = 1e-5
RMS_EPS = 1e-6
NEG = -1e30
M_INIT = -3.0e38
LOG2E = math.log2(math.e)

NN = ((1,), (0,))
NT = ((1,), (1,))
TN = ((0,), (0,))


def _dot(a, b, dims=NN, exact=False):
    dn = (dims, ((), ()))
    if exact:
        return lax.dot_general(a, b, dn, precision=lax.Precision.HIGHEST, preferred_element_type=F32)
    return lax.dot_general(a.astype(BF16), b.astype(BF16), dn, preferred_element_type=F32)


def _split(x):
    hi = x.astype(BF16)
    return hi, (x - hi.astype(F32)).astype(BF16)


def _dot_split(a, b, dims=NN):
    dn = (dims, ((), ()))
    out = lax.dot_general(a[0], b[0], dn, preferred_element_type=F32)
    out = out + lax.dot_general(a[0], b[1], dn, preferred_element_type=F32)
    return out + lax.dot_general(a[1], b[0], dn, preferred_element_type=F32)


def _silu(x):
    return x * (1.0 / (1.0 + jnp.exp(-x)))


def _params(*sem):
    return pltpu.CompilerParams(dimension_semantics=sem, vmem_limit_bytes=VMEM_LIMIT)


def _bucket_thresholds():
    n = np.arange(0, 4 * MAX_DIST, dtype=np.int32)
    ratio = np.log(np.maximum(n, 1).astype(np.float32) / np.float32(MAX_EXACT)) / np.float32(math.log(MAX_DIST / MAX_EXACT))
    large = MAX_EXACT + (ratio * np.float32(N_BUCKETS - MAX_EXACT)).astype(np.int32)
    bucket = np.where(n < MAX_EXACT, n, np.minimum(large, N_BUCKETS - 1))
    return [int(np.argmax(bucket >= b)) for b in range(N_BUCKETS)]


_THR = _bucket_thresholds()


def _bias_kernel(rel_ref, o_ref, *, bases, rows):
    h = pl.program_id(0)
    cols = o_ref.shape[2]
    for i, base in enumerate(bases):
        dist = (base + lax.broadcasted_iota(jnp.int32, (rows, cols), 0)
                - lax.broadcasted_iota(jnp.int32, (rows, cols), 1))
        val = jnp.full((rows, cols), rel_ref[0, h], F32)
        for b in range(1, N_BUCKETS):
            val = jnp.where(dist >= _THR[b], rel_ref[b, h], val)
        o_ref[0, i * rows:(i + 1) * rows, :] = jnp.where(dist >= 0, val, NEG)


def bias_table(rel_bias, bases, rows, cols):
    nh = rel_bias.shape[1]
    return pl.pallas_call(
        functools.partial(_bias_kernel, bases=tuple(bases), rows=rows),
        grid=(nh,),
        in_specs=[pl.BlockSpec(memory_space=pltpu.SMEM)],
        out_specs=pl.BlockSpec((1, len(bases) * rows, cols), lambda h: (h, 0, 0)),
        out_shape=jax.ShapeDtypeStruct((nh, len(bases) * rows, cols), F32),
        compiler_params=_params("arbitrary"),
        name="bias_table",
    )(rel_bias)


def _mm_kernel(x_ref, w_ref, o_ref):
    o_ref[...] = jnp.dot(x_ref[...].astype(BF16), w_ref[...], preferred_element_type=F32)


def matmul(x, w, tm, tn):
    m, k = x.shape
    n = w.shape[1]
    return pl.pallas_call(
        _mm_kernel,
        grid=(n // tn, m // tm),
        in_specs=[pl.BlockSpec((tm, k), lambda j, i: (i, 0)),
                  pl.BlockSpec((k, tn), lambda j, i: (0, j))],
        out_specs=pl.BlockSpec((tm, tn), lambda j, i: (i, j)),
        out_shape=jax.ShapeDtypeStruct((m, n), F32),
        compiler_params=_params("arbitrary", "arbitrary"),
        name="matmul",
    )(x, w)


def _select_topk(gate, valid, pos, k, axis=-1):
    big = float(1 << 20)
    pos = pos.astype(F32)
    g = jnp.where(valid, gate, NEG)
    live_pos = jnp.where(valid, pos, big)
    sel = jnp.zeros(gate.shape, jnp.int32)
    for _ in range(k):
        m = jnp.max(g, axis=axis, keepdims=True)
        idx = jnp.min(jnp.where(g == m, live_pos, big), axis=axis, keepdims=True)
        hit = pos == idx
        sel = jnp.where(hit, 1, sel)
        live_pos = jnp.where(hit, big, live_pos)
        g = jnp.where(hit, NEG, g)
    return sel


def _moba_prompt_kernel(q_ref, k_ref, v_ref, g_ref, bias_ref, o_ref, kaug, vaug, kmat, s_scr, *, nb, group):
    qi = pl.program_id(2)
    blk = MOBA_BLOCK
    lane = lax.broadcasted_iota(jnp.int32, (blk, LANE), 1)

    @pl.when(qi == 0)
    def _():
        s = k_ref.shape[1]
        kmean = jnp.sum(k_ref[0].reshape(nb, blk, LANE), axis=1) * (1.0 / blk)
        lane_nb = lax.broadcasted_iota(jnp.int32, (nb, LANE), 1)
        for hh in range(2):
            free0 = (1 - hh) * DH_A
            kmat[hh] = jnp.zeros((LANE, LANE), F32)
            kmat[hh, free0:free0 + nb, :] = jnp.where(lane_nb // DH_A == hh, kmean, 0.0)

        def fill(j, carry):
            rows = pl.ds(pl.multiple_of(j * blk, blk), blk)
            kj = k_ref[0, rows, :]
            vj = v_ref[0, rows, :]
            for hh in range(2):
                free0 = (1 - hh) * DH_A
                own = lane // DH_A == hh
                kaug[hh, rows, :] = jnp.where(own, kj, jnp.where(lane == free0 + j, 1.0, 0.0)).astype(BF16)
                vaug[hh, rows, :] = jnp.where(own, vj, 1.0).astype(BF16)
            return carry

        lax.fori_loop(0, s // blk, fill, 0)

    q = q_ref[0] * (DH_A ** -0.5)
    blk_row = lax.broadcasted_iota(jnp.int32, (nb, blk), 0)
    sel_bias = {}
    for hh in range(2):
        free0 = (1 - hh) * DH_A
        qh = jnp.where(lane // DH_A == hh, q, 0.0)
        gate_t = _dot(kmat[hh], qh, NT)[free0:free0 + nb]
        sel = _select_topk(gate_t, blk_row < qi, blk_row, MOBA_TOPK, axis=0)
        attend = jnp.where(blk_row == qi, 1, sel)
        sel_bias[free0] = jnp.where(attend > 0, 0.0, NEG)
    gap = jnp.zeros((DH_A - nb, blk), F32)
    bias_lanes = jnp.concatenate([sel_bias[0], gap, sel_bias[DH_A], gap], axis=0).T
    q_aug = [jnp.where(lane // DH_A == hh, q, bias_lanes).astype(BF16) for hh in range(2)]

    ngroups = (qi + group) // group

    def scores(gi, mrun):
        mrun = list(mrun)
        for u in range(group):
            j = gi * group + u
            rows = pl.ds(pl.multiple_of(j * blk, blk), blk)
            for hh in range(2):
                s = _dot(q_aug[hh], kaug[hh, rows, :], NT) + bias_ref[hh, jnp.clip(qi - j, 0, 2)]
                s = s * LOG2E
                s_scr[hh, j] = s
                mrun[hh] = jnp.maximum(mrun[hh], jnp.maximum(s[:, :LANE], s[:, LANE:]))
        return tuple(mrun)

    mrun = lax.fori_loop(0, ngroups, scores, (jnp.full((blk, LANE), M_INIT, F32),) * 2)
    m = [jnp.max(mr, axis=-1, keepdims=True) for mr in mrun]

    def weighted(gi, acc):
        acc = list(acc)
        for u in range(group):
            j = gi * group + u
            rows = pl.ds(pl.multiple_of(j * blk, blk), blk)
            for hh in range(2):
                acc[hh] = acc[hh] + _dot(jnp.exp2(s_scr[hh, j] - m[hh]), vaug[hh, rows, :])
        return tuple(acc)

    acc = lax.fori_loop(0, ngroups, weighted, (jnp.zeros((blk, LANE), F32),) * 2)
    outs = [a / pltpu.roll(a, DH_A, axis=1) for a in acc]
    o_ref[0] = _silu(g_ref[0]) * jnp.where(lane < DH_A, outs[0], outs[1])


def moba_prompt(z, bias):
    bsz, s, _ = z.shape
    nh = bias.shape[0]
    npair = nh // 2
    nb = s // MOBA_BLOCK
    assert s % MOBA_BLOCK == 0 and nb <= DH_A and 2 * DH_A == LANE
    blk = MOBA_BLOCK
    group = next(g for g in (8, 4, 2, 1) if nb % g == 0)
    return pl.pallas_call(
        functools.partial(_moba_prompt_kernel, nb=nb, group=group),
        grid=(bsz, npair, nb),
        in_specs=[pl.BlockSpec((1, blk, LANE), lambda b, p, i: (b, i, p)),
                  pl.BlockSpec((1, s, LANE), lambda b, p, i: (b, 0, npair + p)),
                  pl.BlockSpec((1, s, LANE), lambda b, p, i: (b, 0, 2 * npair + p)),
                  pl.BlockSpec((1, blk, LANE), lambda b, p, i: (b, i, 3 * npair + p)),
                  pl.BlockSpec((2, 3, blk, blk), lambda b, p, i: (p, 0, 0, 0))],
        out_specs=pl.BlockSpec((1, blk, LANE), lambda b, p, i: (b, i, p)),
        out_shape=jax.ShapeDtypeStruct((bsz, s, nh * DH_A), F32),
        scratch_shapes=[pltpu.VMEM((2, s, LANE), BF16), pltpu.VMEM((2, s, LANE), BF16),
                        pltpu.VMEM((2, LANE, LANE), F32), pltpu.VMEM((2, nb, blk, blk), F32)],
        compiler_params=_params("arbitrary", "arbitrary", "arbitrary"),
        name="moba_prompt",
    )(z, z, z, z, bias)


def _moba_sample_kernel(pt_ref, q_ref, kn_ref, vn_ref, g_ref, *refs, nb, nh, bps):
    del pt_ref
    k_refs, v_refs = refs[:2 * bps], refs[2 * bps:4 * bps]
    bias_past_ref, bias_own_ref, o_ref, qexp, acc_scr, m_all, l_all, gate_all = refs[4 * bps:]
    step = pl.program_id(1)
    t8 = SUBLANE
    rows = nh * t8
    width = nh * DH_A
    page = k_refs[0].shape[2]
    row_head = lax.broadcasted_iota(jnp.int32, (rows, width), 0) // t8
    lane_head = lax.broadcasted_iota(jnp.int32, (rows, width), 1) // DH_A
    head_mask = row_head == lane_head
    lane = lax.broadcasted_iota(jnp.int32, (rows, LANE), 1)

    @pl.when(step == 0)
    def _():
        q8 = q_ref[0] * (DH_A ** -0.5)
        qexp[...] = jnp.where(head_mask, jnp.concatenate([q8] * nh, axis=0), 0.0).astype(BF16)
        m_all[...] = jnp.zeros((rows, LANE), F32)
        l_all[...] = jnp.zeros((rows, LANE), F32)
        gate_all[...] = jnp.zeros((rows, LANE), F32)

    qe = qexp[...]
    m_new, l_new, gate_new = m_all[...], l_all[...], gate_all[...]
    blocks = [step * bps + u for u in range(bps)]
    raw = [(_dot(qe, k_refs[2 * u][0]), _dot(qe, k_refs[2 * u + 1][0])) for u in range(bps)]
    gates = [jnp.sum(ra + rb, axis=-1, keepdims=True) * (1.0 / MOBA_BLOCK) for ra, rb in raw]
    bias = [bias_past_ref[jnp.where(j == nb - 1, 0, 1)] for j in blocks]
    s = [(ra + b[:, :page], rb + b[:, page:]) for (ra, rb), b in zip(raw, bias)]
    m_blk = [jnp.max(jnp.maximum(sa, sb), axis=-1, keepdims=True) for sa, sb in s]
    p = [(jnp.exp(sa - m), jnp.exp(sb - m)) for (sa, sb), m in zip(s, m_blk)]
    l_blk = [jnp.sum(pa + pb, axis=-1, keepdims=True) for pa, pb in p]
    pv = [_dot(pa, v_refs[2 * u][0], NT) + _dot(pb, v_refs[2 * u + 1][0], NT) for u, (pa, pb) in enumerate(p)]
    for u, j in enumerate(blocks):
        acc_scr[j] = pv[u]
        m_new = jnp.where(lane == j, m_blk[u], m_new)
        l_new = jnp.where(lane == j, l_blk[u], l_new)
        gate_new = jnp.where(lane == j, gates[u], gate_new)
    m_all[...] = m_new
    l_all[...] = l_new
    gate_all[...] = gate_new

    @pl.when(step == pl.num_programs(1) - 1)
    def _():
        pad = jnp.zeros((LANE - t8, width), F32)
        s_o = _dot(qe, jnp.concatenate([kn_ref[0], pad], axis=0), NT) + bias_own_ref[...]
        m_o = jnp.max(s_o, axis=-1, keepdims=True)
        p_o = jnp.exp(s_o - m_o)
        l_o = jnp.sum(p_o, axis=-1, keepdims=True)
        acc_o = _dot(p_o, jnp.concatenate([vn_ref[0], pad], axis=0))
        sel = _select_topk(gate_all[...], lane < nb, lane, min(MOBA_TOPK, nb)) > 0
        mm = m_all[...]
        m_tot = jnp.maximum(m_o, jnp.max(jnp.where(sel, mm, M_INIT), axis=-1, keepdims=True))
        w_all = jnp.where(sel, jnp.exp(jnp.where(sel, mm, m_tot) - m_tot), 0.0)
        w_o = jnp.exp(m_o - m_tot)
        l_tot = w_o * l_o + jnp.sum(w_all * l_all[...], axis=-1, keepdims=True)
        tot = w_o * acc_o
        for jb in range(nb):
            tot = tot + w_all[:, jb:jb + 1] * acc_scr[jb]
        tot = jnp.where(head_mask, tot / l_tot, 0.0)
        out8 = tot[0:t8]
        for h in range(1, nh):
            out8 = out8 + tot[h * t8:(h + 1) * t8]
        o_ref[0] = _silu(g_ref[0]) * out8


def moba_sample(z, cache_kt, cache_vt, page0, page_table, bias_past, bias_own):
    bs, t8, _ = z.shape
    width, page = cache_kt.shape[1:]
    nh = width // DH_A
    n_pages = page_table.shape[1]
    assert t8 == SUBLANE and MOBA_BLOCK == 2 * page and (n_pages * page) % MOBA_BLOCK == 0
    nb = n_pages * page // MOBA_BLOCK
    assert nb <= LANE
    rows = nh * t8
    bps = next(g for g in (4, 2, 1) if nb % g == 0)
    zspec = lambda c: pl.BlockSpec((1, t8, width), lambda b, j, pt: (b, 0, c))
    pspec = lambda o: pl.BlockSpec((1, width, page), lambda b, j, pt: (page0 + pt[b, 2 * bps * j + o], 0, 0))
    pages = [pspec(o) for o in range(2 * bps)]
    grid_spec = pltpu.PrefetchScalarGridSpec(
        num_scalar_prefetch=1,
        grid=(bs, nb // bps),
        in_specs=[zspec(0), zspec(1), zspec(2), zspec(3)] + pages + pages
                 + [pl.BlockSpec((2, rows, MOBA_BLOCK), lambda b, j, pt: (0, 0, 0)),
                    pl.BlockSpec((rows, LANE), lambda b, j, pt: (0, 0))],
        out_specs=pl.BlockSpec((1, t8, width), lambda b, j, pt: (b, 0, 0)),
        scratch_shapes=[pltpu.VMEM((rows, width), BF16), pltpu.VMEM((nb, rows, width), F32),
                        pltpu.VMEM((rows, LANE), F32), pltpu.VMEM((rows, LANE), F32),
                        pltpu.VMEM((rows, LANE), F32)])
    return pl.pallas_call(
        functools.partial(_moba_sample_kernel, nb=nb, nh=nh, bps=bps),
        grid_spec=grid_spec,
        out_shape=jax.ShapeDtypeStruct((bs, t8, width), F32),
        compiler_params=_params("arbitrary", "arbitrary"),
        name="moba_sample",
    )(page_table, z, z, z, z, *([cache_kt] * (2 * bps)), *([cache_vt] * (2 * bps)), bias_past, bias_own)


def _conv_b_kernel(b_ref, c_ref, h_ref, g_ref, prev_ref, w_ref, y_ref, tail_ref, buf):
    r = c_ref.shape[1]

    @pl.when(pl.program_id(1) == 0)
    def _():
        buf[0:SUBLANE, :] = prev_ref[0]

    u = c_ref[0] * h_ref[0]
    buf[SUBLANE:SUBLANE + r, :] = u
    conv = u * w_ref[CONV_B - 1:CONV_B, :]
    for jtap in range(CONV_B - 1):
        off = SUBLANE - (CONV_B - 1) + jtap
        conv = conv + buf[off:off + r, :] * w_ref[jtap:jtap + 1, :]
    y_ref[0] = _silu(g_ref[0]) * (b_ref[0] * conv)
    tail_ref[0] = u[r - SUBLANE:, :]
    buf[0:SUBLANE, :] = u[r - SUBLANE:, :]


def conv_b(z, prev8, w8, tr, col0):
    bsz, t, _ = z.shape
    width = prev8.shape[2]
    zspec = lambda c: pl.BlockSpec((1, tr, width), lambda b, i: (b, i, c))
    return pl.pallas_call(
        _conv_b_kernel,
        grid=(bsz, t // tr),
        in_specs=[zspec(col0), zspec(col0 + 1), zspec(col0 + 2), zspec(col0 + 3),
                  pl.BlockSpec((1, SUBLANE, width), lambda b, i: (b, 0, 0)),
                  pl.BlockSpec((SUBLANE, width), lambda b, i: (0, 0))],
        out_specs=[pl.BlockSpec((1, tr, width), lambda b, i: (b, i, 0)),
                   pl.BlockSpec((1, SUBLANE, width), lambda b, i: (b, 0, 0))],
        out_shape=[jax.ShapeDtypeStruct((bsz, t, width), F32),
                   jax.ShapeDtypeStruct((bsz, SUBLANE, width), F32)],
        scratch_shapes=[pltpu.VMEM((tr + SUBLANE, width), F32)],
        compiler_params=_params("arbitrary", "arbitrary"),
        name="conv_b",
    )(z, z, z, z, prev8, w8)


def _head_sums(x, ones_ref):
    hi, lo = _split(x)
    ones = ones_ref[...]
    return jnp.dot(hi, ones, preferred_element_type=F32) + jnp.dot(lo, ones, preferred_element_type=F32)


def _gdn_pre_kernel(xq_ref, xk_ref, xv_ref, pq_ref, pk_ref, pv_ref, w_ref, ones_ref, q_ref, k_ref, v_ref, buf):
    r = xq_ref.shape[1]
    width = xq_ref.shape[2]
    first = pl.program_id(1) == 0
    for idx, (x_ref, p_ref, o_ref) in enumerate(((xq_ref, pq_ref, q_ref), (xk_ref, pk_ref, k_ref),
                                                 (xv_ref, pv_ref, v_ref))):
        @pl.when(first)
        def _(idx=idx, p_ref=p_ref):
            buf[idx, 0:SUBLANE, :] = p_ref[0]

        x = x_ref[0]
        buf[idx, SUBLANE:SUBLANE + r, :] = x
        wcol = slice(idx * width, (idx + 1) * width)
        conv = x * w_ref[CONV_C - 1:CONV_C, wcol]
        for jtap in range(CONV_C - 1):
            off = SUBLANE - (CONV_C - 1) + jtap
            conv = conv + buf[idx, off:off + r, :] * w_ref[jtap:jtap + 1, wcol]
        buf[idx, 0:SUBLANE, :] = x[r - SUBLANE:, :]
        act = _silu(conv)
        if idx < 2:
            act = act * lax.rsqrt(_head_sums(act * act, ones_ref) + RMS_EPS)
        o_ref[0] = act


def _head_ones(width, head):
    ids = np.arange(width) // head
    return jnp.asarray(ids[:, None] == ids[None, :], BF16)


def gdn_pre(z, prev8, w8, tr, col0):
    bsz, t, _ = z.shape
    width = prev8.shape[2] // 3
    zspec = lambda c: pl.BlockSpec((1, tr, width), lambda b, i: (b, i, c))
    pspec = lambda c: pl.BlockSpec((1, SUBLANE, width), lambda b, i: (b, 0, c))
    ospec = pl.BlockSpec((1, tr, width), lambda b, i: (b, i, 0))
    oshape = jax.ShapeDtypeStruct((bsz, t, width), F32)
    return pl.pallas_call(
        _gdn_pre_kernel,
        grid=(bsz, t // tr),
        in_specs=[zspec(col0), zspec(col0 + 1), zspec(col0 + 2), pspec(0), pspec(1), pspec(2),
                  pl.BlockSpec((SUBLANE, 3 * width), lambda b, i: (0, 0)),
                  pl.BlockSpec((width, width), lambda b, i: (0, 0))],
        out_specs=[ospec, ospec, ospec],
        out_shape=[oshape, oshape, oshape],
        scratch_shapes=[pltpu.VMEM((3, tr + SUBLANE, width), F32)],
        compiler_params=_params("arbitrary", "arbitrary"),
        name="gdn_pre",
    )(z, z, z, prev8, prev8, prev8, w8, _head_ones(width, DK_C))


def _gdn_chunk_kernel(q_ref, k_ref, v_ref, ab_ref, alog_ref, dtb_ref, s0_ref, o_ref, sout_ref, s_scr,
                      *, nh, t_valid):
    c = GDN_CHUNK
    ti = pl.program_id(1)
    nseq, tc = q_ref.shape[:2]

    @pl.when(ti == 0)
    def _():
        s_scr[...] = s0_ref[...]

    row = lax.broadcasted_iota(jnp.int32, (c, c), 0)
    col = lax.broadcasted_iota(jnp.int32, (c, c), 1)
    tri = row >= col
    strict = row > col
    eye = jnp.where(row == col, 1.0, 0.0)
    tril_ones = jnp.where(tri, 1.0, 0.0)
    neg_a = -jnp.exp(alog_ref[...])
    dtb = dtb_ref[...]

    def chunk(ci, carry):
        r0 = pl.multiple_of(ci * c, c)
        rows = pl.ds(r0, c)
        gc_all, gc_t, beta_all = [], [], []
        for b in range(nseq):
            ab = ab_ref[b, rows, :]
            g_b = neg_a * (jnp.maximum(ab + dtb, 0.0) + jnp.log(1.0 + jnp.exp(-jnp.abs(ab + dtb))))
            beta_b = 1.0 / (1.0 + jnp.exp(-ab))
            if t_valid is not None:
                live = (ti * tc + r0 + lax.broadcasted_iota(jnp.int32, (c, LANE), 0)) < t_valid
                g_b = jnp.where(live, g_b, 0.0)
                beta_b = jnp.where(live, beta_b, 0.0)
            gc_b = _dot(tril_ones, g_b, NN, exact=True)
            gc_all.append(gc_b)
            gc_t.append(gc_b.T)
            beta_all.append(beta_b)
        units = [(b, h) for b in range(nseq) for h in range(nh)]
        heads = range(len(units))
        gcol = [gc_all[b][:, h:h + 1] for b, h in units]
        bcol = [beta_all[b][:, nh + h:nh + h + 1] for b, h in units]
        decay = [jnp.where(tri, jnp.exp(jnp.where(tri, gcol[u] - gc_t[b][h:h + 1, :], 0.0)), 0.0)
                 for u, (b, h) in enumerate(units)]
        head_cols = lambda x, h: x[:, h * DK_C:(h + 1) * DK_C]
        q_all = [q_ref[b, rows, :] * (DK_C ** -0.5) for b in range(nseq)]
        k_all = [k_ref[b, rows, :] for b in range(nseq)]
        v_all = [v_ref[b, rows, :] for b in range(nseq)]
        q = [head_cols(q_all[b], h) for b, h in units]
        k = [head_cols(k_all[b], h) for b, h in units]
        v = [head_cols(v_all[b], h) for b, h in units]
        kbeta = [k[h] * bcol[h] for h in heads]
        a_mat = [jnp.where(strict, _dot(kbeta[h], k[h], NT) * decay[h], 0.0) for h in heads]
        tinv = [eye - a for a in a_mat]
        power = [_split(a) for a in a_mat]
        for _ in range(int(math.log2(c)) - 1):
            power = [_split(_dot_split(p, p)) for p in power]
            tinv = [t + _dot_split(_split(t), p) for t, p in zip(tinv, power)]
        egc = [jnp.exp(g) for g in gcol]
        u = [_dot(tinv[h], v[h] * bcol[h]) for h in heads]
        w = [_dot(tinv[h], kbeta[h] * egc[h]) for h in heads]
        s_old = [s_scr[b, h] for b, h in units]
        v_new = [u[h] - _dot(w[h], s_old[h]) for h in heads]
        attn = [_dot(q[h], k[h], NT) * decay[h] for h in heads]
        o = [_dot(q[h] * egc[h], s_old[h]) + _dot(attn[h], v_new[h]) for h in heads]
        for b in range(nseq):
            o_ref[b, rows, :] = jnp.concatenate(o[b * nh:(b + 1) * nh], axis=-1)
        for i, (b, h) in enumerate(units):
            g_last = gc_all[b][c - 1:c, h:h + 1]
            s_scr[b, h] = s_old[i] * jnp.exp(g_last) + _dot(k[i] * jnp.exp(g_last - gcol[i]), v_new[i], TN)
        return carry

    lax.fori_loop(0, tc // c, chunk, 0)

    @pl.when(ti == pl.num_programs(1) - 1)
    def _():
        sout_ref[...] = s_scr[...]


def gdn_chunk(q, k, v, ab, ab_col, alog_lane, dtb_lane, s0, tc, t_valid):
    bsz, t, width = q.shape
    nh = width // DK_C
    assert t % tc == 0 and tc % GDN_CHUNK == 0 and DK_C == DV_C
    nseq = 2 if bsz % 2 == 0 else 1
    qspec = pl.BlockSpec((nseq, tc, width), lambda b, i: (b, i, 0))
    sspec = pl.BlockSpec((nseq, nh, DK_C, DV_C), lambda b, i: (b, 0, 0, 0))
    vec = pl.BlockSpec((1, LANE), lambda b, i: (0, 0))
    return pl.pallas_call(
        functools.partial(_gdn_chunk_kernel, nh=nh, t_valid=None if t_valid == t else t_valid),
        grid=(bsz // nseq, t // tc),
        in_specs=[qspec, qspec, qspec, pl.BlockSpec((nseq, tc, LANE), lambda b, i: (b, i, ab_col)), vec, vec, sspec],
        out_specs=[qspec, sspec],
        out_shape=[jax.ShapeDtypeStruct((bsz, t, width), F32),
                   jax.ShapeDtypeStruct((bsz, nh, DK_C, DV_C), F32)],
        scratch_shapes=[pltpu.VMEM((nseq, nh, DK_C, DV_C), F32)],
        compiler_params=_params("arbitrary", "arbitrary"),
        name="gdn_chunk",
    )(q, k, v, ab, alog_lane, dtb_lane, s0)


def _layer_norm(hid, g_ref, b_ref):
    mu = jnp.mean(hid, axis=-1, keepdims=True)
    cen = hid - mu
    var = jnp.mean(cen * cen, axis=-1, keepdims=True)
    return cen * lax.rsqrt(var + LN_EPS) * g_ref[...] + b_ref[...]


def _layer_tail_kernel(ya_ref, yb_ref, oc_ref, gc_ref, nw_ref, ones_ref, wa_ref, wb_ref, wc_ref, x_ref, g1_ref, b1_ref,
                       wq_ref, mk_ref, mv_ref, wo_ref, g2_ref, b2_ref, o_ref, *, alpha):
    nseq, tr, d = x_ref.shape
    flat = lambda ref: ref[...].reshape(nseq * tr, ref.shape[2])
    oc = flat(oc_ref)
    yc = _silu(flat(gc_ref)) * (oc * lax.rsqrt(_head_sums(oc * oc, ones_ref) * (1.0 / DV_C) + RMS_EPS) * nw_ref[...])
    y = _dot(flat(ya_ref), wa_ref[...]) + _dot(flat(yb_ref), wb_ref[...]) + _dot(yc, wc_ref[...])
    x1 = _layer_norm(alpha * flat(x_ref) + y, g1_ref, b1_ref)
    q = _dot(x1, wq_ref[...]) * ((d // H_X) ** -0.5)
    dh = d // H_X
    units = [(b, slice(h * dh, (h + 1) * dh)) for b in range(nseq) for h in range(H_X)]
    s = [_dot(q[b * tr:(b + 1) * tr, cols], mk_ref[b, :, cols], NT) for b, cols in units]
    p = [jnp.exp(x - jnp.max(x, axis=-1, keepdims=True)) for x in s]
    p = [x / jnp.sum(x, axis=-1, keepdims=True) for x in p]
    pv = [_dot(x, mv_ref[b, :, cols]) for x, (b, cols) in zip(p, units)]
    ctx = [jnp.concatenate(pv[b * H_X:(b + 1) * H_X], axis=-1) for b in range(nseq)]
    ctx = ctx[0] if nseq == 1 else jnp.concatenate(ctx, axis=0)
    out = _layer_norm(alpha * x1 + _dot(ctx, wo_ref[...]), g2_ref, b2_ref)
    o_ref[...] = out.reshape(nseq, tr, d)


def layer_tail(ya, yb, oc, z, gate_col, norm_w, w_out, x, ln1, wq, mk, mv, row0, wo, ln2, alpha, tr):
    bsz, t, d = x.shape
    nm = mk.shape[1]
    widths = (ya.shape[2], yb.shape[2], oc.shape[2])
    offs = (0, widths[0], widths[0] + widths[1])
    nseq = next(n for n in (8, 4, 2, 1) if bsz % n == 0 and row0 % n == 0 and n * tr <= 256)
    row = lambda w: pl.BlockSpec((nseq, tr, w), lambda b, i: (b, i, 0))
    full = lambda a: pl.BlockSpec(a.shape, lambda b, i: (0,) * a.ndim)
    mspec = pl.BlockSpec((nseq, nm, d), lambda b, i: (row0 // nseq + b, 0, 0))
    w_parts = [w_out[o:o + w] for o, w in zip(offs, widths)]
    vecs = [v.reshape(1, d) for v in (*ln1, *ln2)]
    nw = jnp.tile(norm_w, widths[2] // DV_C).reshape(1, widths[2])
    ones = _head_ones(widths[2], DV_C)
    return pl.pallas_call(
        functools.partial(_layer_tail_kernel, alpha=alpha),
        grid=(bsz // nseq, t // tr),
        in_specs=[row(widths[0]), row(widths[1]), row(widths[2]),
                  pl.BlockSpec((nseq, tr, widths[2]), lambda b, i: (b, i, gate_col)), full(nw), full(ones)]
                 + [full(w) for w in w_parts]
                 + [row(d), full(vecs[0]), full(vecs[1]), full(wq), mspec, mspec, full(wo), full(vecs[2]), full(vecs[3])],
        out_specs=row(d),
        out_shape=jax.ShapeDtypeStruct((bsz, t, d), F32),
        compiler_params=_params("arbitrary", "arbitrary"),
        name="layer_tail",
    )(ya, yb, oc, z, nw, ones, *w_parts, x, vecs[0], vecs[1], wq, mk, mv, wo, vecs[2], vecs[3])


def _pad_rows(a, rows, axis=1, front=False):
    pad = [(0, 0)] * a.ndim
    pad[axis] = (rows - a.shape[axis], 0) if front else (0, rows - a.shape[axis])
    return jnp.pad(a, pad)


def _tile(t, pref):
    return pref if t % pref == 0 else t


def _layer(x, t_valid, attn_fn, conv_b_prev, conv_c_prev, s0, mk, mv, mem_row0, lw):
    bsz, t, d = x.shape
    m = bsz * t
    w_b = lw["conv_b_w"].shape[1]
    w_c = lw["conv_c_w"].shape[1] // 3
    nh_c = w_c // DV_C
    w_a = lw["w_a"]
    assert w_a == w_b == w_c and w_a % LANE == 0, "column blocks of z are addressed in units of one group width"
    tm = _tile(m, 512)
    tr = _tile(t, 256)

    z = matmul(x.reshape(m, d), lw["w_in"], tm, lw["w_in"].shape[1] // lw["n_split"]).reshape(bsz, t, -1)
    ya = attn_fn(z)
    yb, tail_b = conv_b(z, _pad_rows(conv_b_prev, SUBLANE, front=True), _pad_rows(lw["conv_b_w"], SUBLANE, axis=0),
                        tr, 4)
    qc, kc, vc = gdn_pre(z, _pad_rows(conv_c_prev, SUBLANE, front=True), _pad_rows(lw["conv_c_w"], SUBLANE, axis=0),
                         tr, 8)
    ab_col = (12 * w_a) // LANE
    if t % GDN_CHUNK == 0:
        o, s_new = gdn_chunk(qc, kc, vc, z, ab_col, lw["alog_lane"], lw["dtb_lane"], s0, _tile(t, 512), t_valid)
    else:
        tp = GDN_CHUNK
        padt = lambda a: _pad_rows(a, tp, axis=1)
        ab = padt(z[:, :, 12 * w_a:12 * w_a + LANE])
        o, s_new = gdn_chunk(padt(qc), padt(kc), padt(vc), ab, 0, lw["alog_lane"], lw["dtb_lane"], s0, tp, t_valid)
        o = o[:, :t]

    x3 = layer_tail(ya, yb, o, z, 11, lw["norm_w"], lw["w_out"], x, (lw["ln1_g"], lw["ln1_b"]), lw["wq_x"], mk, mv,
                    mem_row0, lw["wo_x"], (lw["ln2_g"], lw["ln2_b"]), lw["alpha"], tr)

    nh_a = w_a // DH_A
    k_new = z[:, :t_valid, w_a:2 * w_a].reshape(bsz, t_valid, nh_a, DH_A)
    v_new = z[:, :t_valid, 2 * w_a:3 * w_a].reshape(bsz, t_valid, nh_a, DH_A)
    tail0 = t - SUBLANE
    conv_b_new = tail_b[:, t_valid - tail0 - (CONV_B - 1):t_valid - tail0]
    conv_c_new = z[:, t_valid - (CONV_C - 1):t_valid, 8 * w_a:11 * w_a]
    return x3, k_new, v_new, conv_b_new, conv_c_new, s_new


def kernel(x_prompt, x_sample, mem_prompt, cache_attn_k, cache_attn_v, cache_mem_k, cache_mem_v, state_conv_b, state_conv_c, state_delta, page_table, w_in, conv_b_w, conv_c_w, gdn_a_log, gdn_dt_bias, gdn_norm_w, w_out, ln1_g, ln1_b, rel_bias, wq_x, wk_x, wv_x, wo_x, ln2_g, ln2_b):
    depth, d, n_in = w_in.shape
    bp, seq, _ = x_prompt.shape
    bs, dec_seq, _ = x_sample.shape
    nh_a = rel_bias.shape[1]
    w_a = nh_a * DH_A
    nh_c = gdn_a_log.shape[1]
    n_mem = mem_prompt.shape[1]
    page = cache_attn_k.shape[2]
    past = page_table.shape[1] * page
    alpha = float((2 * depth) ** 0.25)
    assert n_in == 12 * w_a + 2 * nh_c and dec_seq <= SUBLANE and 2 * nh_c <= LANE
    assert seq >= CONV_C - 1 and dec_seq >= CONV_C - 1 and past % MOBA_BLOCK == 0

    nz = 12 * w_a + 2 * LANE
    n_split = 2
    lane_vec = lambda v: _pad_rows(v.reshape(1, -1).astype(F32), LANE, axis=1)

    bias_prompt = bias_table(rel_bias, (0, MOBA_BLOCK, 2 * MOBA_BLOCK), MOBA_BLOCK, MOBA_BLOCK)
    bias_prompt = bias_prompt.reshape(nh_a, 3, MOBA_BLOCK, MOBA_BLOCK)
    nbf = past // MOBA_BLOCK
    bias_past = bias_table(rel_bias, (past - (nbf - 1) * MOBA_BLOCK, past - (nbf - 2) * MOBA_BLOCK), SUBLANE, MOBA_BLOCK)
    bias_past = bias_past.reshape(nh_a, 2, SUBLANE, MOBA_BLOCK).transpose(1, 0, 2, 3).reshape(2, nh_a * SUBLANE, MOBA_BLOCK)
    bias_own = bias_table(rel_bias, (0,), SUBLANE, LANE).reshape(nh_a * SUBLANE, LANE)

    n_phys = cache_attn_k.shape[1]
    cache_kt = cache_attn_k.transpose(0, 1, 3, 4, 2).reshape(depth * n_phys, w_a, page)
    cache_vt = cache_attn_v.transpose(0, 1, 3, 4, 2).reshape(depth * n_phys, w_a, page)

    mem_k_all = cache_mem_k.reshape(depth * bs, n_mem, d)
    mem_v_all = cache_mem_v.reshape(depth * bs, n_mem, d)

    xp = x_prompt
    xs = _pad_rows(x_sample, SUBLANE)
    outs = [[] for _ in range(12)]
    for l in range(depth):
        lw = dict(
            w_in=_pad_rows(w_in[l], nz, axis=1).astype(BF16), n_split=n_split, w_a=w_a,
            conv_b_w=conv_b_w[l], conv_c_w=conv_c_w[l],
            alog_lane=lane_vec(gdn_a_log[l]), dtb_lane=lane_vec(gdn_dt_bias[l]), norm_w=gdn_norm_w[l],
            w_out=w_out[l].astype(BF16), ln1_g=ln1_g[l], ln1_b=ln1_b[l],
            wq_x=wq_x[l].astype(BF16), wo_x=wo_x[l].astype(BF16), ln2_g=ln2_g[l], ln2_b=ln2_b[l],
            alpha=alpha)
        mem2 = mem_prompt.reshape(bp * n_mem, d)
        tmem = _tile(bp * n_mem, 512)
        mk = matmul(mem2, wk_x[l].astype(BF16), tmem, d).reshape(bp, n_mem, d)
        mv = matmul(mem2, wv_x[l].astype(BF16), tmem, d).reshape(bp, n_mem, d)

        xp, k, v, cb, cc, sd = _layer(
            xp, seq, functools.partial(moba_prompt, bias=bias_prompt),
            jnp.zeros((bp, CONV_B - 1, conv_b_w.shape[2]), F32), jnp.zeros((bp, CONV_C - 1, conv_c_w.shape[2]), F32),
            jnp.zeros((bp, nh_c, DK_C, DV_C), F32), mk, mv, 0, lw)
        for lst, val in zip(outs[:2] + outs[4:6] + [outs[6], outs[8], outs[10]],
                            (k, v, mk.reshape(bp, n_mem, H_X, d // H_X), mv.reshape(bp, n_mem, H_X, d // H_X),
                             cb, cc, sd)):
            lst.append(val)

        attn_s = functools.partial(
            moba_sample, cache_kt=cache_kt, cache_vt=cache_vt, page0=l * n_phys,
            page_table=page_table, bias_past=bias_past, bias_own=bias_own)
        xs, k, v, cb, cc, sd = _layer(
            xs, dec_seq, attn_s, state_conv_b[l], state_conv_c[l], state_delta[l],
            mem_k_all, mem_v_all, l * bs, lw)
        for lst, val in zip(outs[2:4] + [outs[7], outs[9], outs[11]], (k, v, cb, cc, sd)):
            lst.append(val)

    kp, vp, ksm, vsm, mkp, mvp, cbp, cbs, ccp, ccs, sdp, sds = (jnp.stack(o) for o in outs)
    return (xp, xs[:, :dec_seq], kp, vp, ksm, vsm, mkp, mvp, cbp, cbs, ccp, ccs, sdp, sds)
```

```python
import functools
import math

import numpy as np
import jax
import jax.numpy as jnp
from jax import lax
from jax.experimental import pallas as pl
from jax.experimental.pallas import tpu as pltpu

F32 = jnp.float32
BF16 = jnp.bfloat16

LANE = 128
SUBLANE = 8
VMEM_LIMIT = 56 * 1024 * 1024

DH_A = 64
MOBA_BLOCK = 256
MOBA_TOPK = 3
N_BUCKETS = 32
MAX_EXACT = 16
MAX_DIST = 128
CONV_B = 3
CONV_C = 4
DK_C = 64
DV_C = 64
GDN_CHUNK = 64
H_X = 4
LN_EPS = 1e-5
RMS_EPS = 1e-6
NEG = -1e30
M_INIT = -3.0e38
LOG2E = math.log2(math.e)

NN = ((1,), (0,))
NT = ((1,), (1,))
TN = ((0,), (0,))


def _dot(a, b, dims=NN, exact=False):
    dn = (dims, ((), ()))
    if exact:
        return lax.dot_general(a, b, dn, precision=lax.Precision.HIGHEST, preferred_element_type=F32)
    return lax.dot_general(a.astype(BF16), b.astype(BF16), dn, preferred_element_type=F32)


def _split(x):
    hi = x.astype(BF16)
    return hi, (x - hi.astype(F32)).astype(BF16)


def _dot_split(a, b, dims=NN):
    dn = (dims, ((), ()))
    out = lax.dot_general(a[0], b[0], dn, preferred_element_type=F32)
    out = out + lax.dot_general(a[0], b[1], dn, preferred_element_type=F32)
    return out + lax.dot_general(a[1], b[0], dn, preferred_element_type=F32)


def _silu(x):
    return x * (1.0 / (1.0 + jnp.exp(-x)))


def _params(*sem):
    return pltpu.CompilerParams(dimension_semantics=sem, vmem_limit_bytes=VMEM_LIMIT)


def _bucket_thresholds():
    n = np.arange(0, 4 * MAX_DIST, dtype=np.int32)
    ratio = np.log(np.maximum(n, 1).astype(np.float32) / np.float32(MAX_EXACT)) / np.float32(math.log(MAX_DIST / MAX_EXACT))
    large = MAX_EXACT + (ratio * np.float32(N_BUCKETS - MAX_EXACT)).astype(np.int32)
    bucket = np.where(n < MAX_EXACT, n, np.minimum(large, N_BUCKETS - 1))
    return [int(np.argmax(bucket >= b)) for b in range(N_BUCKETS)]


_THR = _bucket_thresholds()


def _bias_kernel(rel_ref, o_ref, *, bases, rows):
    h = pl.program_id(0)
    cols = o_ref.shape[2]
    for i, base in enumerate(bases):
        dist = (base + lax.broadcasted_iota(jnp.int32, (rows, cols), 0)
                - lax.broadcasted_iota(jnp.int32, (rows, cols), 1))
        val = jnp.full((rows, cols), rel_ref[0, h], F32)
        for b in range(1, N_BUCKETS):
            val = jnp.where(dist >= _THR[b], rel_ref[b, h], val)
        o_ref[0, i * rows:(i + 1) * rows, :] = jnp.where(dist >= 0, val, NEG)


def bias_table(rel_bias, bases, rows, cols):
    nh = rel_bias.shape[1]
    return pl.pallas_call(
        functools.partial(_bias_kernel, bases=tuple(bases), rows=rows),
        grid=(nh,),
        in_specs=[pl.BlockSpec(memory_space=pltpu.SMEM)],
        out_specs=pl.BlockSpec((1, len(bases) * rows, cols), lambda h: (h, 0, 0)),
        out_shape=jax.ShapeDtypeStruct((nh, len(bases) * rows, cols), F32),
        compiler_params=_params("arbitrary"),
        name="bias_table",
    )(rel_bias)


def _mm_kernel(x_ref, w_ref, o_ref):
    o_ref[...] = jnp.dot(x_ref[...].astype(BF16), w_ref[...], preferred_element_type=F32)


def matmul(x, w, tm, tn):
    m, k = x.shape
    n = w.shape[1]
    return pl.pallas_call(
        _mm_kernel,
        grid=(n // tn, m // tm),
        in_specs=[pl.BlockSpec((tm, k), lambda j, i: (i, 0)),
                  pl.BlockSpec((k, tn), lambda j, i: (0, j))],
        out_specs=pl.BlockSpec((tm, tn), lambda j, i: (i, j)),
        out_shape=jax.ShapeDtypeStruct((m, n), F32),
        compiler_params=_params("arbitrary", "arbitrary"),
        name="matmul",
    )(x, w)


def _select_topk_many(gates, valids, pos, k, axis=-1):
    big = float(1 << 20)
    pos = pos.astype(F32)
    n = range(len(gates))
    g = [jnp.where(valids[i], gates[i], NEG) for i in n]
    live_pos = [jnp.where(valids[i], pos, big) for i in n]
    sel = [jnp.zeros(gates[i].shape, jnp.int32) for i in n]
    for _ in range(k):
        m = [jnp.max(g[i], axis=axis, keepdims=True) for i in n]
        idx = [jnp.min(jnp.where(g[i] == m[i], live_pos[i], big), axis=axis, keepdims=True) for i in n]
        hit = [pos == idx[i] for i in n]
        sel = [jnp.where(hit[i], 1, sel[i]) for i in n]
        live_pos = [jnp.where(hit[i], big, live_pos[i]) for i in n]
        g = [jnp.where(hit[i], NEG, g[i]) for i in n]
    return sel


def _select_topk(gate, valid, pos, k, axis=-1):
    return _select_topk_many([gate], [valid], pos, k, axis)[0]


def _moba_prompt_kernel(q_ref, k_ref, v_ref, g_ref, bias_ref, o_ref, kaug, vaug, kmat, s_scr, *, nb, group,
                        tiles_per_step):
    qi = pl.program_id(2)
    blk = MOBA_BLOCK
    lane = lax.broadcasted_iota(jnp.int32, (blk, LANE), 1)

    @pl.when(qi == 0)
    def _():
        s = k_ref.shape[1]
        kmean = jnp.sum(k_ref[0].reshape(nb, blk, LANE), axis=1) * (1.0 / blk)
        lane_nb = lax.broadcasted_iota(jnp.int32, (nb, LANE), 1)
        for hh in range(2):
            free0 = (1 - hh) * DH_A
            kmat[hh] = jnp.zeros((LANE, LANE), F32)
            kmat[hh, free0:free0 + nb, :] = jnp.where(lane_nb // DH_A == hh, kmean, 0.0)

        def fill(j, carry):
            rows = pl.ds(pl.multiple_of(j * blk, blk), blk)
            kj = k_ref[0, rows, :]
            vj = v_ref[0, rows, :]
            for hh in range(2):
                free0 = (1 - hh) * DH_A
                own = lane // DH_A == hh
                kaug[hh, rows, :] = jnp.where(own, kj, jnp.where(lane == free0 + j, 1.0, 0.0)).astype(BF16)
                vaug[hh, rows, :] = jnp.where(own, vj, 1.0).astype(BF16)
            return carry

        lax.fori_loop(0, s // blk, fill, 0)

    tiles = [qi * tiles_per_step + t for t in range(tiles_per_step)]
    q = [q_ref[0, t * blk:(t + 1) * blk, :] * (DH_A ** -0.5) for t in range(tiles_per_step)]
    blk_row = lax.broadcasted_iota(jnp.int32, (nb, blk), 0)
    units = [(t, hh) for t in range(tiles_per_step) for hh in range(2)]
    gate_t = [_dot(kmat[hh], jnp.where(lane // DH_A == hh, q[t], 0.0), NT)[(1 - hh) * DH_A:(1 - hh) * DH_A + nb]
              for t, hh in units]
    sel = _select_topk_many(gate_t, [blk_row < tiles[t] for t, _ in units], blk_row, MOBA_TOPK, axis=0)
    sel_bias = [jnp.where(jnp.where(blk_row == tiles[t], 1, s) > 0, 0.0, NEG) for s, (t, _) in zip(sel, units)]
    gap = jnp.zeros((DH_A - nb, blk), F32)
    lane_head = lane // DH_A

    for t, tile in enumerate(tiles):
        bias_lanes = jnp.concatenate([sel_bias[2 * t + 1], gap, sel_bias[2 * t], gap], axis=0).T
        q_aug = [jnp.where(lane_head == hh, q[t], bias_lanes).astype(BF16) for hh in range(2)]

        ngroups = (tile + group) // group

        def scores(gi, mrun, tile=tile, q_aug=q_aug):
            mrun = list(mrun)
            for u in range(group):
                j = gi * group + u
                rows = pl.ds(pl.multiple_of(j * blk, blk), blk)
                for hh in range(2):
                    s = _dot(q_aug[hh], kaug[hh, rows, :], NT) + bias_ref[hh, jnp.clip(tile - j, 0, 2)]
                    s = s * LOG2E
                    s_scr[hh, j] = s
                    mrun[hh] = jnp.maximum(mrun[hh], jnp.maximum(s[:, :LANE], s[:, LANE:]))
            return tuple(mrun)

        mrun = lax.fori_loop(0, ngroups, scores, (jnp.full((blk, LANE), M_INIT, F32),) * 2)
        m = [jnp.max(mr, axis=-1, keepdims=True) for mr in mrun]

        def weighted(gi, acc, m=m):
            acc = list(acc)
            for u in range(group):
                j = gi * group + u
                rows = pl.ds(pl.multiple_of(j * blk, blk), blk)
                for hh in range(2):
                    acc[hh] = acc[hh] + _dot(jnp.exp2(s_scr[hh, j] - m[hh]), vaug[hh, rows, :])
            return tuple(acc)

        acc = lax.fori_loop(0, ngroups, weighted, (jnp.zeros((blk, LANE), F32),) * 2)
        outs = [a / pltpu.roll(a, DH_A, axis=1) for a in acc]
        gate = g_ref[0, t * blk:(t + 1) * blk, :]
        o_ref[0, t * blk:(t + 1) * blk, :] = _silu(gate) * jnp.where(lane < DH_A, outs[0], outs[1])


def moba_prompt(z, bias):
    bsz, s, _ = z.shape
    nh = bias.shape[0]
    npair = nh // 2
    nb = s // MOBA_BLOCK
    assert s % MOBA_BLOCK == 0 and nb <= DH_A and 2 * DH_A == LANE
    blk = MOBA_BLOCK
    group = next(g for g in (8, 4, 2, 1) if nb % g == 0)
    tps = next(g for g in (4, 2, 1) if nb % g == 0)
    return pl.pallas_call(
        functools.partial(_moba_prompt_kernel, nb=nb, group=group, tiles_per_step=tps),
        grid=(bsz, npair, nb // tps),
        in_specs=[pl.BlockSpec((1, tps * blk, LANE), lambda b, p, i: (b, i, p)),
                  pl.BlockSpec((1, s, LANE), lambda b, p, i: (b, 0, npair + p)),
                  pl.BlockSpec((1, s, LANE), lambda b, p, i: (b, 0, 2 * npair + p)),
                  pl.BlockSpec((1, tps * blk, LANE), lambda b, p, i: (b, i, 3 * npair + p)),
                  pl.BlockSpec((2, 3, blk, blk), lambda b, p, i: (p, 0, 0, 0))],
        out_specs=pl.BlockSpec((1, tps * blk, LANE), lambda b, p, i: (b, i, p)),
        out_shape=jax.ShapeDtypeStruct((bsz, s, nh * DH_A), F32),
        scratch_shapes=[pltpu.VMEM((2, s, LANE), BF16), pltpu.VMEM((2, s, LANE), BF16),
                        pltpu.VMEM((2, LANE, LANE), F32), pltpu.VMEM((2, nb, blk, blk), F32)],
        compiler_params=_params("arbitrary", "arbitrary", "arbitrary"),
        name="moba_prompt",
    )(z, z, z, z, bias)


def _moba_sample_kernel(pt_ref, q_ref, kn_ref, vn_ref, g_ref, *refs, nb, nh, bps):
    del pt_ref
    k_refs, v_refs = refs[:2 * bps], refs[2 * bps:4 * bps]
    bias_past_ref, bias_own_ref, o_ref, qexp, acc_scr, m_all, l_all, gate_all = refs[4 * bps:]
    step = pl.program_id(1)
    t8 = SUBLANE
    rows = nh * t8
    width = nh * DH_A
    page = k_refs[0].shape[2]
    row_head = lax.broadcasted_iota(jnp.int32, (rows, width), 0) // t8
    lane_head = lax.broadcasted_iota(jnp.int32, (rows, width), 1) // DH_A
    head_mask = row_head == lane_head
    lane = lax.broadcasted_iota(jnp.int32, (rows, LANE), 1)

    @pl.when(step == 0)
    def _():
        q8 = q_ref[0] * (DH_A ** -0.5)
        qexp[...] = jnp.where(head_mask, jnp.concatenate([q8] * nh, axis=0), 0.0).astype(BF16)
        m_all[...] = jnp.zeros((rows, LANE), F32)
        l_all[...] = jnp.zeros((rows, LANE), F32)
        gate_all[...] = jnp.zeros((rows, LANE), F32)

    qe = qexp[...]
    m_new, l_new, gate_new = m_all[...], l_all[...], gate_all[...]
    blocks = [step * bps + u for u in range(bps)]
    raw = [(_dot(qe, k_refs[2 * u][0]), _dot(qe, k_refs[2 * u + 1][0])) for u in range(bps)]
    gates = [jnp.sum(ra + rb, axis=-1, keepdims=True) * (1.0 / MOBA_BLOCK) for ra, rb in raw]
    bias = [bias_past_ref[jnp.where(j == nb - 1, 0, 1)] for j in blocks]
    s = [(ra + b[:, :page], rb + b[:, page:]) for (ra, rb), b in zip(raw, bias)]
    m_blk = [jnp.max(jnp.maximum(sa, sb), axis=-1, keepdims=True) for sa, sb in s]
    p = [(jnp.exp(sa - m), jnp.exp(sb - m)) for (sa, sb), m in zip(s, m_blk)]
    l_blk = [jnp.sum(pa + pb, axis=-1, keepdims=True) for pa, pb in p]
    pv = [_dot(pa, v_refs[2 * u][0], NT) + _dot(pb, v_refs[2 * u + 1][0], NT) for u, (pa, pb) in enumerate(p)]
    for u, j in enumerate(blocks):
        acc_scr[j] = pv[u]
        m_new = jnp.where(lane == j, m_blk[u], m_new)
        l_new = jnp.where(lane == j, l_blk[u], l_new)
        gate_new = jnp.where(lane == j, gates[u], gate_new)
    m_all[...] = m_new
    l_all[...] = l_new
    gate_all[...] = gate_new

    @pl.when(step == pl.num_programs(1) - 1)
    def _():
        pad = jnp.zeros((LANE - t8, width), F32)
        s_o = _dot(qe, jnp.concatenate([kn_ref[0], pad], axis=0), NT) + bias_own_ref[...]
        m_o = jnp.max(s_o, axis=-1, keepdims=True)
        p_o = jnp.exp(s_o - m_o)
        l_o = jnp.sum(p_o, axis=-1, keepdims=True)
        acc_o = _dot(p_o, jnp.concatenate([vn_ref[0], pad], axis=0))
        sel = _select_topk(gate_all[...], lane < nb, lane, min(MOBA_TOPK, nb)) > 0
        mm = m_all[...]
        m_tot = jnp.maximum(m_o, jnp.max(jnp.where(sel, mm, M_INIT), axis=-1, keepdims=True))
        w_all = jnp.where(sel, jnp.exp(jnp.where(sel, mm, m_tot) - m_tot), 0.0)
        w_o = jnp.exp(m_o - m_tot)
        l_tot = w_o * l_o + jnp.sum(w_all * l_all[...], axis=-1, keepdims=True)
        tot = w_o * acc_o
        for jb in range(nb):
            tot = tot + w_all[:, jb:jb + 1] * acc_scr[jb]
        tot = jnp.where(head_mask, tot / l_tot, 0.0)
        out8 = tot[0:t8]
        for h in range(1, nh):
            out8 = out8 + tot[h * t8:(h + 1) * t8]
        o_ref[0] = _silu(g_ref[0]) * out8


def moba_sample(z, cache_kt, cache_vt, page0, page_table, bias_past, bias_own):
    bs, t8, _ = z.shape
    width, page = cache_kt.shape[1:]
    nh = width // DH_A
    n_pages = page_table.shape[1]
    assert t8 == SUBLANE and MOBA_BLOCK == 2 * page and (n_pages * page) % MOBA_BLOCK == 0
    nb = n_pages * page // MOBA_BLOCK
    assert nb <= LANE
    rows = nh * t8
    bps = next(g for g in (8, 4, 2, 1) if nb % g == 0)
    zspec = lambda c: pl.BlockSpec((1, t8, width), lambda b, j, pt: (b, 0, c))
    pspec = lambda o: pl.BlockSpec((1, width, page), lambda b, j, pt: (page0 + pt[b, 2 * bps * j + o], 0, 0))
    pages = [pspec(o) for o in range(2 * bps)]
    grid_spec = pltpu.PrefetchScalarGridSpec(
        num_scalar_prefetch=1,
        grid=(bs, nb // bps),
        in_specs=[zspec(0), zspec(1), zspec(2), zspec(3)] + pages + pages
                 + [pl.BlockSpec((2, rows, MOBA_BLOCK), lambda b, j, pt: (0, 0, 0)),
                    pl.BlockSpec((rows, LANE), lambda b, j, pt: (0, 0))],
        out_specs=pl.BlockSpec((1, t8, width), lambda b, j, pt: (b, 0, 0)),
        scratch_shapes=[pltpu.VMEM((rows, width), BF16), pltpu.VMEM((nb, rows, width), F32),
                        pltpu.VMEM((rows, LANE), F32), pltpu.VMEM((rows, LANE), F32),
                        pltpu.VMEM((rows, LANE), F32)])
    return pl.pallas_call(
        functools.partial(_moba_sample_kernel, nb=nb, nh=nh, bps=bps),
        grid_spec=grid_spec,
        out_shape=jax.ShapeDtypeStruct((bs, t8, width), F32),
        compiler_params=_params("arbitrary", "arbitrary"),
        name="moba_sample",
    )(page_table, z, z, z, z, *([cache_kt] * (2 * bps)), *([cache_vt] * (2 * bps)), bias_past, bias_own)


def _conv_b_kernel(b_ref, c_ref, h_ref, g_ref, prev_ref, w_ref, y_ref, tail_ref, buf):
    r = c_ref.shape[1]

    @pl.when(pl.program_id(1) == 0)
    def _():
        buf[0:SUBLANE, :] = prev_ref[0]

    u = c_ref[0] * h_ref[0]
    buf[SUBLANE:SUBLANE + r, :] = u
    conv = u * w_ref[CONV_B - 1:CONV_B, :]
    for jtap in range(CONV_B - 1):
        off = SUBLANE - (CONV_B - 1) + jtap
        conv = conv + buf[off:off + r, :] * w_ref[jtap:jtap + 1, :]
    y_ref[0] = _silu(g_ref[0]) * (b_ref[0] * conv)
    tail_ref[0] = u[r - SUBLANE:, :]
    buf[0:SUBLANE, :] = u[r - SUBLANE:, :]


def conv_b(z, prev8, w8, tr, col0):
    bsz, t, _ = z.shape
    width = prev8.shape[2]
    zspec = lambda c: pl.BlockSpec((1, tr, width), lambda b, i: (b, i, c))
    return pl.pallas_call(
        _conv_b_kernel,
        grid=(bsz, t // tr),
        in_specs=[zspec(col0), zspec(col0 + 1), zspec(col0 + 2), zspec(col0 + 3),
                  pl.BlockSpec((1, SUBLANE, width), lambda b, i: (b, 0, 0)),
                  pl.BlockSpec((SUBLANE, width), lambda b, i: (0, 0))],
        out_specs=[pl.BlockSpec((1, tr, width), lambda b, i: (b, i, 0)),
                   pl.BlockSpec((1, SUBLANE, width), lambda b, i: (b, 0, 0))],
        out_shape=[jax.ShapeDtypeStruct((bsz, t, width), F32),
                   jax.ShapeDtypeStruct((bsz, SUBLANE, width), F32)],
        scratch_shapes=[pltpu.VMEM((tr + SUBLANE, width), F32)],
        compiler_params=_params("arbitrary", "arbitrary"),
        name="conv_b",
    )(z, z, z, z, prev8, w8)


def _head_sums(x, ones_ref):
    hi, lo = _split(x)
    ones = ones_ref[...]
    return jnp.dot(hi, ones, preferred_element_type=F32) + jnp.dot(lo, ones, preferred_element_type=F32)


def _gdn_pre_kernel(xq_ref, xk_ref, xv_ref, pq_ref, pk_ref, pv_ref, w_ref, ones_ref, q_ref, k_ref, v_ref, buf):
    r = xq_ref.shape[1]
    width = xq_ref.shape[2]
    first = pl.program_id(1) == 0
    for idx, (x_ref, p_ref, o_ref) in enumerate(((xq_ref, pq_ref, q_ref), (xk_ref, pk_ref, k_ref),
                                                 (xv_ref, pv_ref, v_ref))):
        @pl.when(first)
        def _(idx=idx, p_ref=p_ref):
            buf[idx, 0:SUBLANE, :] = p_ref[0]

        x = x_ref[0]
        buf[idx, SUBLANE:SUBLANE + r, :] = x
        wcol = slice(idx * width, (idx + 1) * width)
        conv = x * w_ref[CONV_C - 1:CONV_C, wcol]
        for jtap in range(CONV_C - 1):
            off = SUBLANE - (CONV_C - 1) + jtap
            conv = conv + buf[idx, off:off + r, :] * w_ref[jtap:jtap + 1, wcol]
        buf[idx, 0:SUBLANE, :] = x[r - SUBLANE:, :]
        act = _silu(conv)
        if idx < 2:
            act = act * lax.rsqrt(_head_sums(act * act, ones_ref) + RMS_EPS)
        o_ref[0] = act


def _head_ones(width, head):
    ids = np.arange(width) // head
    return jnp.asarray(ids[:, None] == ids[None, :], BF16)


def gdn_pre(z, prev8, w8, tr, col0):
    bsz, t, _ = z.shape
    width = prev8.shape[2] // 3
    zspec = lambda c: pl.BlockSpec((1, tr, width), lambda b, i: (b, i, c))
    pspec = lambda c: pl.BlockSpec((1, SUBLANE, width), lambda b, i: (b, 0, c))
    ospec = pl.BlockSpec((1, tr, width), lambda b, i: (b, i, 0))
    oshape = jax.ShapeDtypeStruct((bsz, t, width), F32)
    return pl.pallas_call(
        _gdn_pre_kernel,
        grid=(bsz, t // tr),
        in_specs=[zspec(col0), zspec(col0 + 1), zspec(col0 + 2), pspec(0), pspec(1), pspec(2),
                  pl.BlockSpec((SUBLANE, 3 * width), lambda b, i: (0, 0)),
                  pl.BlockSpec((width, width), lambda b, i: (0, 0))],
        out_specs=[ospec, ospec, ospec],
        out_shape=[oshape, oshape, oshape],
        scratch_shapes=[pltpu.VMEM((3, tr + SUBLANE, width), F32)],
        compiler_params=_params("arbitrary", "arbitrary"),
        name="gdn_pre",
    )(z, z, z, prev8, prev8, prev8, w8, _head_ones(width, DK_C))


def _gdn_chunk_kernel(q_ref, k_ref, v_ref, ab_ref, alog_ref, dtb_ref, s0_ref, o_ref, sout_ref, s_scr,
                      *, nh, t_valid):
    c = GDN_CHUNK
    ti = pl.program_id(1)
    nseq, tc = q_ref.shape[:2]

    @pl.when(ti == 0)
    def _():
        s_scr[...] = s0_ref[...]

    row = lax.broadcasted_iota(jnp.int32, (c, c), 0)
    col = lax.broadcasted_iota(jnp.int32, (c, c), 1)
    tri = row >= col
    strict = row > col
    eye = jnp.where(row == col, 1.0, 0.0)
    tril_ones = jnp.where(tri, 1.0, 0.0)
    neg_a = -jnp.exp(alog_ref[...])
    dtb = dtb_ref[...]

    def chunk(ci, carry):
        r0 = pl.multiple_of(ci * c, c)
        rows = pl.ds(r0, c)
        gc_all, gc_t, beta_all = [], [], []
        for b in range(nseq):
            ab = ab_ref[b, rows, :]
            g_b = neg_a * (jnp.maximum(ab + dtb, 0.0) + jnp.log(1.0 + jnp.exp(-jnp.abs(ab + dtb))))
            beta_b = 1.0 / (1.0 + jnp.exp(-ab))
            if t_valid is not None:
                live = (ti * tc + r0 + lax.broadcasted_iota(jnp.int32, (c, LANE), 0)) < t_valid
                g_b = jnp.where(live, g_b, 0.0)
                beta_b = jnp.where(live, beta_b, 0.0)
            gc_b = _dot(tril_ones, g_b, NN, exact=True)
            gc_all.append(gc_b)
            gc_t.append(gc_b.T)
            beta_all.append(beta_b)
        units = [(b, h) for b in range(nseq) for h in range(nh)]
        heads = range(len(units))
        gcol = [gc_all[b][:, h:h + 1] for b, h in units]
        bcol = [beta_all[b][:, nh + h:nh + h + 1] for b, h in units]
        decay = [jnp.where(tri, jnp.exp(jnp.where(tri, gcol[u] - gc_t[b][h:h + 1, :], 0.0)), 0.0)
                 for u, (b, h) in enumerate(units)]
        head_cols = lambda x, h: x[:, h * DK_C:(h + 1) * DK_C]
        q_all = [q_ref[b, rows, :] * (DK_C ** -0.5) for b in range(nseq)]
        k_all = [k_ref[b, rows, :] for b in range(nseq)]
        v_all = [v_ref[b, rows, :] for b in range(nseq)]
        q = [head_cols(q_all[b], h) for b, h in units]
        k = [head_cols(k_all[b], h) for b, h in units]
        v = [head_cols(v_all[b], h) for b, h in units]
        kbeta = [k[h] * bcol[h] for h in heads]
        a_mat = [jnp.where(strict, _dot(kbeta[h], k[h], NT) * decay[h], 0.0) for h in heads]
        tinv = [eye - a for a in a_mat]
        power = [_split(a) for a in a_mat]
        for _ in range(int(math.log2(c)) - 1):
            power = [_split(_dot_split(p, p)) for p in power]
            tinv = [t + _dot_split(_split(t), p) for t, p in zip(tinv, power)]
        egc = [jnp.exp(g) for g in gcol]
        u = [_dot(tinv[h], v[h] * bcol[h]) for h in heads]
        w = [_dot(tinv[h], kbeta[h] * egc[h]) for h in heads]
        s_old = [s_scr[b, h] for b, h in units]
        v_new = [u[h] - _dot(w[h], s_old[h]) for h in heads]
        attn = [_dot(q[h], k[h], NT) * decay[h] for h in heads]
        o = [_dot(q[h] * egc[h], s_old[h]) + _dot(attn[h], v_new[h]) for h in heads]
        for b in range(nseq):
            o_ref[b, rows, :] = jnp.concatenate(o[b * nh:(b + 1) * nh], axis=-1)
        for i, (b, h) in enumerate(units):
            g_last = gc_all[b][c - 1:c, h:h + 1]
            s_scr[b, h] = s_old[i] * jnp.exp(g_last) + _dot(k[i] * jnp.exp(g_last - gcol[i]), v_new[i], TN)
        return carry

    lax.fori_loop(0, tc // c, chunk, 0)

    @pl.when(ti == pl.num_programs(1) - 1)
    def _():
        sout_ref[...] = s_scr[...]


def gdn_chunk(q, k, v, ab, ab_col, alog_lane, dtb_lane, s0, tc, t_valid):
    bsz, t, width = q.shape
    nh = width // DK_C
    assert t % tc == 0 and tc % GDN_CHUNK == 0 and DK_C == DV_C
    nseq = 2 if bsz % 2 == 0 else 1
    qspec = pl.BlockSpec((nseq, tc, width), lambda b, i: (b, i, 0))
    sspec = pl.BlockSpec((nseq, nh, DK_C, DV_C), lambda b, i: (b, 0, 0, 0))
    vec = pl.BlockSpec((1, LANE), lambda b, i: (0, 0))
    return pl.pallas_call(
        functools.partial(_gdn_chunk_kernel, nh=nh, t_valid=None if t_valid == t else t_valid),
        grid=(bsz // nseq, t // tc),
        in_specs=[qspec, qspec, qspec, pl.BlockSpec((nseq, tc, LANE), lambda b, i: (b, i, ab_col)), vec, vec, sspec],
        out_specs=[qspec, sspec],
        out_shape=[jax.ShapeDtypeStruct((bsz, t, width), F32),
                   jax.ShapeDtypeStruct((bsz, nh, DK_C, DV_C), F32)],
        scratch_shapes=[pltpu.VMEM((nseq, nh, DK_C, DV_C), F32)],
        compiler_params=_params("arbitrary", "arbitrary"),
        name="gdn_chunk",
    )(q, k, v, ab, alog_lane, dtb_lane, s0)


def _layer_norm(hid, g_ref, b_ref):
    mu = jnp.mean(hid, axis=-1, keepdims=True)
    cen = hid - mu
    var = jnp.mean(cen * cen, axis=-1, keepdims=True)
    return cen * lax.rsqrt(var + LN_EPS) * g_ref[...] + b_ref[...]


def _layer_tail_kernel(ya_ref, yb_ref, oc_ref, gc_ref, nw_ref, ones_ref, wa_ref, wb_ref, wc_ref, x_ref, g1_ref, b1_ref,
                       wq_ref, mk_ref, mv_ref, wo_ref, g2_ref, b2_ref, o_ref, *, alpha):
    nseq, tr, d = x_ref.shape
    flat = lambda ref: ref[...].reshape(nseq * tr, ref.shape[2])
    oc = flat(oc_ref)
    yc = _silu(flat(gc_ref)) * (oc * lax.rsqrt(_head_sums(oc * oc, ones_ref) * (1.0 / DV_C) + RMS_EPS) * nw_ref[...])
    y = _dot(flat(ya_ref), wa_ref[...]) + _dot(flat(yb_ref), wb_ref[...]) + _dot(yc, wc_ref[...])
    x1 = _layer_norm(alpha * flat(x_ref) + y, g1_ref, b1_ref)
    q = _dot(x1, wq_ref[...]) * ((d // H_X) ** -0.5)
    dh = d // H_X
    units = [(b, slice(h * dh, (h + 1) * dh)) for b in range(nseq) for h in range(H_X)]
    s = [_dot(q[b * tr:(b + 1) * tr, cols], mk_ref[b, :, cols], NT) for b, cols in units]
    p = [jnp.exp(x - jnp.max(x, axis=-1, keepdims=True)) for x in s]
    p = [x / jnp.sum(x, axis=-1, keepdims=True) for x in p]
    pv = [_dot(x, mv_ref[b, :, cols]) for x, (b, cols) in zip(p, units)]
    ctx = [jnp.concatenate(pv[b * H_X:(b + 1) * H_X], axis=-1) for b in range(nseq)]
    ctx = ctx[0] if nseq == 1 else jnp.concatenate(ctx, axis=0)
    out = _layer_norm(alpha * x1 + _dot(ctx, wo_ref[...]), g2_ref, b2_ref)
    o_ref[...] = out.reshape(nseq, tr, d)


def layer_tail(ya, yb, oc, z, gate_col, norm_w, w_out, x, ln1, wq, mk, mv, row0, wo, ln2, alpha, tr):
    bsz, t, d = x.shape
    nm = mk.shape[1]
    widths = (ya.shape[2], yb.shape[2], oc.shape[2])
    offs = (0, widths[0], widths[0] + widths[1])
    nseq = next(n for n in (8, 4, 2, 1) if bsz % n == 0 and row0 % n == 0 and n * tr <= 256)
    row = lambda w: pl.BlockSpec((nseq, tr, w), lambda b, i: (b, i, 0))
    full = lambda a: pl.BlockSpec(a.shape, lambda b, i: (0,) * a.ndim)
    mspec = pl.BlockSpec((nseq, nm, d), lambda b, i: (row0 // nseq + b, 0, 0))
    w_parts = [w_out[o:o + w] for o, w in zip(offs, widths)]
    vecs = [v.reshape(1, d) for v in (*ln1, *ln2)]
    nw = jnp.tile(norm_w, widths[2] // DV_C).reshape(1, widths[2])
    ones = _head_ones(widths[2], DV_C)
    return pl.pallas_call(
        functools.partial(_layer_tail_kernel, alpha=alpha),
        grid=(bsz // nseq, t // tr),
        in_specs=[row(widths[0]), row(widths[1]), row(widths[2]),
                  pl.BlockSpec((nseq, tr, widths[2]), lambda b, i: (b, i, gate_col)), full(nw), full(ones)]
                 + [full(w) for w in w_parts]
                 + [row(d), full(vecs[0]), full(vecs[1]), full(wq), mspec, mspec, full(wo), full(vecs[2]), full(vecs[3])],
        out_specs=row(d),
        out_shape=jax.ShapeDtypeStruct((bsz, t, d), F32),
        compiler_params=_params("arbitrary", "arbitrary"),
        name="layer_tail",
    )(ya, yb, oc, z, nw, ones, *w_parts, x, vecs[0], vecs[1], wq, mk, mv, wo, vecs[2], vecs[3])


def _pad_rows(a, rows, axis=1, front=False):
    pad = [(0, 0)] * a.ndim
    pad[axis] = (rows - a.shape[axis], 0) if front else (0, rows - a.shape[axis])
    return jnp.pad(a, pad)


def _tile(t, pref):
    return pref if t % pref == 0 else t


def _layer(x, t_valid, attn_fn, conv_b_prev, conv_c_prev, s0, mk, mv, mem_row0, lw):
    bsz, t, d = x.shape
    m = bsz * t
    w_b = lw["conv_b_w"].shape[1]
    w_c = lw["conv_c_w"].shape[1] // 3
    nh_c = w_c // DV_C
    w_a = lw["w_a"]
    assert w_a == w_b == w_c and w_a % LANE == 0, "column blocks of z are addressed in units of one group width"
    tm = _tile(m, 512)
    tr = _tile(t, 256)

    z = matmul(x.reshape(m, d), lw["w_in"], tm, lw["w_in"].shape[1] // lw["n_split"]).reshape(bsz, t, -1)
    ya = attn_fn(z)
    yb, tail_b = conv_b(z, _pad_rows(conv_b_prev, SUBLANE, front=True), _pad_rows(lw["conv_b_w"], SUBLANE, axis=0),
                        tr, 4)
    qc, kc, vc = gdn_pre(z, _pad_rows(conv_c_prev, SUBLANE, front=True), _pad_rows(lw["conv_c_w"], SUBLANE, axis=0),
                         tr, 8)
    ab_col = (12 * w_a) // LANE
    if t % GDN_CHUNK == 0:
        o, s_new = gdn_chunk(qc, kc, vc, z, ab_col, lw["alog_lane"], lw["dtb_lane"], s0, _tile(t, 512), t_valid)
    else:
        tp = GDN_CHUNK
        padt = lambda a: _pad_rows(a, tp, axis=1)
        ab = padt(z[:, :, 12 * w_a:12 * w_a + LANE])
        o, s_new = gdn_chunk(padt(qc), padt(kc), padt(vc), ab, 0, lw["alog_lane"], lw["dtb_lane"], s0, tp, t_valid)
        o = o[:, :t]

    x3 = layer_tail(ya, yb, o, z, 11, lw["norm_w"], lw["w_out"], x, (lw["ln1_g"], lw["ln1_b"]), lw["wq_x"], mk, mv,
                    mem_row0, lw["wo_x"], (lw["ln2_g"], lw["ln2_b"]), lw["alpha"], tr)

    nh_a = w_a // DH_A
    k_new = z[:, :t_valid, w_a:2 * w_a].reshape(bsz, t_valid, nh_a, DH_A)
    v_new = z[:, :t_valid, 2 * w_a:3 * w_a].reshape(bsz, t_valid, nh_a, DH_A)
    tail0 = t - SUBLANE
    conv_b_new = tail_b[:, t_valid - tail0 - (CONV_B - 1):t_valid - tail0]
    conv_c_new = z[:, t_valid - (CONV_C - 1):t_valid, 8 * w_a:11 * w_a]
    return x3, k_new, v_new, conv_b_new, conv_c_new, s_new


def kernel(x_prompt, x_sample, mem_prompt, cache_attn_k, cache_attn_v, cache_mem_k, cache_mem_v, state_conv_b, state_conv_c, state_delta, page_table, w_in, conv_b_w, conv_c_w, gdn_a_log, gdn_dt_bias, gdn_norm_w, w_out, ln1_g, ln1_b, rel_bias, wq_x, wk_x, wv_x, wo_x, ln2_g, ln2_b):
    depth, d, n_in = w_in.shape
    bp, seq, _ = x_prompt.shape
    bs, dec_seq, _ = x_sample.shape
    nh_a = rel_bias.shape[1]
    w_a = nh_a * DH_A
    nh_c = gdn_a_log.shape[1]
    n_mem = mem_prompt.shape[1]
    page = cache_attn_k.shape[2]
    past = page_table.shape[1] * page
    alpha = float((2 * depth) ** 0.25)
    assert n_in == 12 * w_a + 2 * nh_c and dec_seq <= SUBLANE and 2 * nh_c <= LANE
    assert seq >= CONV_C - 1 and dec_seq >= CONV_C - 1 and past % MOBA_BLOCK == 0

    nz = 12 * w_a + 2 * LANE
    n_split = 2
    lane_vec = lambda v: _pad_rows(v.reshape(1, -1).astype(F32), LANE, axis=1)

    bias_prompt = bias_table(rel_bias, (0, MOBA_BLOCK, 2 * MOBA_BLOCK), MOBA_BLOCK, MOBA_BLOCK)
    bias_prompt = bias_prompt.reshape(nh_a, 3, MOBA_BLOCK, MOBA_BLOCK)
    nbf = past // MOBA_BLOCK
    bias_past = bias_table(rel_bias, (past - (nbf - 1) * MOBA_BLOCK, past - (nbf - 2) * MOBA_BLOCK), SUBLANE, MOBA_BLOCK)
    bias_past = bias_past.reshape(nh_a, 2, SUBLANE, MOBA_BLOCK).transpose(1, 0, 2, 3).reshape(2, nh_a * SUBLANE, MOBA_BLOCK)
    bias_own = bias_table(rel_bias, (0,), SUBLANE, LANE).reshape(nh_a * SUBLANE, LANE)

    n_phys = cache_attn_k.shape[1]
    cache_kt = cache_attn_k.transpose(0, 1, 3, 4, 2).reshape(depth * n_phys, w_a, page)
    cache_vt = cache_attn_v.transpose(0, 1, 3, 4, 2).reshape(depth * n_phys, w_a, page)

    mem_k_all = cache_mem_k.reshape(depth * bs, n_mem, d)
    mem_v_all = cache_mem_v.reshape(depth * bs, n_mem, d)

    xp = x_prompt
    xs = _pad_rows(x_sample, SUBLANE)
    outs = [[] for _ in range(12)]
    for l in range(depth):
        lw = dict(
            w_in=_pad_rows(w_in[l], nz, axis=1).astype(BF16), n_split=n_split, w_a=w_a,
            conv_b_w=conv_b_w[l], conv_c_w=conv_c_w[l],
            alog_lane=lane_vec(gdn_a_log[l]), dtb_lane=lane_vec(gdn_dt_bias[l]), norm_w=gdn_norm_w[l],
            w_out=w_out[l].astype(BF16), ln1_g=ln1_g[l], ln1_b=ln1_b[l],
            wq_x=wq_x[l].astype(BF16), wo_x=wo_x[l].astype(BF16), ln2_g=ln2_g[l], ln2_b=ln2_b[l],
            alpha=alpha)
        mem2 = mem_prompt.reshape(bp * n_mem, d)
        tmem = _tile(bp * n_mem, 512)
        mk = matmul(mem2, wk_x[l].astype(BF16), tmem, d).reshape(bp, n_mem, d)
        mv = matmul(mem2, wv_x[l].astype(BF16), tmem, d).reshape(bp, n_mem, d)

        xp, k, v, cb, cc, sd = _layer(
            xp, seq, functools.partial(moba_prompt, bias=bias_prompt),
            jnp.zeros((bp, CONV_B - 1, conv_b_w.shape[2]), F32), jnp.zeros((bp, CONV_C - 1, conv_c_w.shape[2]), F32),
            jnp.zeros((bp, nh_c, DK_C, DV_C), F32), mk, mv, 0, lw)
        for lst, val in zip(outs[:2] + outs[4:6] + [outs[6], outs[8], outs[10]],
                            (k, v, mk.reshape(bp, n_mem, H_X, d // H_X), mv.reshape(bp, n_mem, H_X, d // H_X),
                             cb, cc, sd)):
            lst.append(val)

        attn_s = functools.partial(
            moba_sample, cache_kt=cache_kt, cache_vt=cache_vt, page0=l * n_phys,
            page_table=page_table, bias_past=bias_past, bias_own=bias_own)
        xs, k, v, cb, cc, sd = _layer(
            xs, dec_seq, attn_s, state_conv_b[l], state_conv_c[l], state_delta[l],
            mem_k_all, mem_v_all, l * bs, lw)
        for lst, val in zip(outs[2:4] + [outs[7], outs[9], outs[11]], (k, v, cb, cc, sd)):
            lst.append(val)

    kp, vp, ksm, vsm, mkp, mvp, cbp, cbs, ccp, ccs, sdp, sds = (jnp.stack(o) for o in outs)
    return (xp, xs[:, :dec_seq], kp, vp, ksm, vsm, mkp, mvp, cbp, cbs, ccp, ccs, sdp, sds)
```

```python
import functools
import math

import numpy as np
import jax
import jax.numpy as jnp
from jax import lax
from jax.experimental import pallas as pl
from jax.experimental.pallas import tpu as pltpu

F32 = jnp.float32
BF16 = jnp.bfloat16

LANE = 128
SUBLANE = 8
VMEM_LIMIT = 56 * 1024 * 1024

DH_A = 64
MOBA_BLOCK = 256
MOBA_TOPK = 3
N_BUCKETS = 32
MAX_EXACT = 16
MAX_DIST = 128
CONV_B = 3
CONV_C = 4
DK_C = 64
DV_C = 64
GDN_CHUNK = 64
H_X = 4
LN_EPS = 1e-5
RMS_EPS = 1e-6
NEG = -1e30
M_INIT = -3.0e38
LOG2E = math.log2(math.e)

NN = ((1,), (0,))
NT = ((1,), (1,))
TN = ((0,), (0,))


def _dot(a, b, dims=NN, exact=False):
    dn = (dims, ((), ()))
    if exact:
        return lax.dot_general(a, b, dn, precision=lax.Precision.HIGHEST, preferred_element_type=F32)
    return lax.dot_general(a.astype(BF16), b.astype(BF16), dn, preferred_element_type=F32)


def _split(x):
    hi = x.astype(BF16)
    return hi, (x - hi.astype(F32)).astype(BF16)


def _dot_split(a, b, dims=NN):
    dn = (dims, ((), ()))
    out = lax.dot_general(a[0], b[0], dn, preferred_element_type=F32)
    out = out + lax.dot_general(a[0], b[1], dn, preferred_element_type=F32)
    return out + lax.dot_general(a[1], b[0], dn, preferred_element_type=F32)


def _silu(x):
    return x * (1.0 / (1.0 + jnp.exp(-x)))


def _params(*sem):
    return pltpu.CompilerParams(dimension_semantics=sem, vmem_limit_bytes=VMEM_LIMIT)


def _bucket_thresholds():
    n = np.arange(0, 4 * MAX_DIST, dtype=np.int32)
    ratio = np.log(np.maximum(n, 1).astype(np.float32) / np.float32(MAX_EXACT)) / np.float32(math.log(MAX_DIST / MAX_EXACT))
    large = MAX_EXACT + (ratio * np.float32(N_BUCKETS - MAX_EXACT)).astype(np.int32)
    bucket = np.where(n < MAX_EXACT, n, np.minimum(large, N_BUCKETS - 1))
    return [int(np.argmax(bucket >= b)) for b in range(N_BUCKETS)]


_THR = _bucket_thresholds()


def _bias_kernel(rel_ref, o_ref, *, bases, rows):
    h = pl.program_id(0)
    cols = o_ref.shape[2]
    for i, base in enumerate(bases):
        dist = (base + lax.broadcasted_iota(jnp.int32, (rows, cols), 0)
                - lax.broadcasted_iota(jnp.int32, (rows, cols), 1))
        val = jnp.full((rows, cols), rel_ref[0, h], F32)
        for b in range(1, N_BUCKETS):
            val = jnp.where(dist >= _THR[b], rel_ref[b, h], val)
        o_ref[0, i * rows:(i + 1) * rows, :] = jnp.where(dist >= 0, val, NEG)


def bias_table(rel_bias, bases, rows, cols):
    nh = rel_bias.shape[1]
    return pl.pallas_call(
        functools.partial(_bias_kernel, bases=tuple(bases), rows=rows),
        grid=(nh,),
        in_specs=[pl.BlockSpec(memory_space=pltpu.SMEM)],
        out_specs=pl.BlockSpec((1, len(bases) * rows, cols), lambda h: (h, 0, 0)),
        out_shape=jax.ShapeDtypeStruct((nh, len(bases) * rows, cols), F32),
        compiler_params=_params("arbitrary"),
        name="bias_table",
    )(rel_bias)


def _mm_kernel(x_ref, w_ref, o_ref):
    o_ref[...] = jnp.dot(x_ref[...].astype(BF16), w_ref[...], preferred_element_type=F32)


def matmul(x, w, tm, tn):
    m, k = x.shape
    n = w.shape[1]
    return pl.pallas_call(
        _mm_kernel,
        grid=(n // tn, m // tm),
        in_specs=[pl.BlockSpec((tm, k), lambda j, i: (i, 0)),
                  pl.BlockSpec((k, tn), lambda j, i: (0, j))],
        out_specs=pl.BlockSpec((tm, tn), lambda j, i: (i, j)),
        out_shape=jax.ShapeDtypeStruct((m, n), F32),
        compiler_params=_params("arbitrary", "arbitrary"),
        name="matmul",
    )(x, w)


def _select_topk_many(gates, valids, pos, k, axis=-1):
    big = float(1 << 20)
    pos = pos.astype(F32)
    n = range(len(gates))
    g = [jnp.where(valids[i], gates[i], NEG) for i in n]
    live_pos = [jnp.where(valids[i], pos, big) for i in n]
    sel = [jnp.zeros(gates[i].shape, jnp.int32) for i in n]
    for _ in range(k):
        m = [jnp.max(g[i], axis=axis, keepdims=True) for i in n]
        idx = [jnp.min(jnp.where(g[i] == m[i], live_pos[i], big), axis=axis, keepdims=True) for i in n]
        hit = [pos == idx[i] for i in n]
        sel = [jnp.where(hit[i], 1, sel[i]) for i in n]
        live_pos = [jnp.where(hit[i], big, live_pos[i]) for i in n]
        g = [jnp.where(hit[i], NEG, g[i]) for i in n]
    return sel


def _select_topk(gate, valid, pos, k, axis=-1):
    return _select_topk_many([gate], [valid], pos, k, axis)[0]


def _moba_prompt_kernel(q_ref, k_ref, v_ref, g_ref, bias_ref, o_ref, kaug, vaug, kmat, s_scr, *, nb, group,
                        tiles_per_step):
    qi = pl.program_id(2)
    blk = MOBA_BLOCK
    lane = lax.broadcasted_iota(jnp.int32, (blk, LANE), 1)

    @pl.when(qi == 0)
    def _():
        s = k_ref.shape[1]
        kmean = jnp.sum(k_ref[0].reshape(nb, blk, LANE), axis=1) * (1.0 / blk)
        lane_nb = lax.broadcasted_iota(jnp.int32, (nb, LANE), 1)
        for hh in range(2):
            free0 = (1 - hh) * DH_A
            kmat[hh] = jnp.zeros((LANE, LANE), F32)
            kmat[hh, free0:free0 + nb, :] = jnp.where(lane_nb // DH_A == hh, kmean, 0.0)

        def fill(j, carry):
            rows = pl.ds(pl.multiple_of(j * blk, blk), blk)
            kj = k_ref[0, rows, :]
            vj = v_ref[0, rows, :]
            for hh in range(2):
                free0 = (1 - hh) * DH_A
                own = lane // DH_A == hh
                kaug[hh, rows, :] = jnp.where(own, kj, jnp.where(lane == free0 + j, 1.0, 0.0)).astype(BF16)
                vaug[hh, rows, :] = jnp.where(own, vj, 1.0).astype(BF16)
            return carry

        lax.fori_loop(0, s // blk, fill, 0)

    tiles = [qi * tiles_per_step + t for t in range(tiles_per_step)]
    q = [q_ref[0, t * blk:(t + 1) * blk, :] * (DH_A ** -0.5) for t in range(tiles_per_step)]
    blk_row = lax.broadcasted_iota(jnp.int32, (nb, blk), 0)
    units = [(t, hh) for t in range(tiles_per_step) for hh in range(2)]
    gate_t = [_dot(kmat[hh], jnp.where(lane // DH_A == hh, q[t], 0.0), NT)[(1 - hh) * DH_A:(1 - hh) * DH_A + nb]
              for t, hh in units]
    sel = _select_topk_many(gate_t, [blk_row < tiles[t] for t, _ in units], blk_row, MOBA_TOPK, axis=0)
    sel_bias = [jnp.where(jnp.where(blk_row == tiles[t], 1, s) > 0, 0.0, NEG) for s, (t, _) in zip(sel, units)]
    gap = jnp.zeros((DH_A - nb, blk), F32)
    lane_head = lane // DH_A

    for t, tile in enumerate(tiles):
        bias_lanes = jnp.concatenate([sel_bias[2 * t + 1], gap, sel_bias[2 * t], gap], axis=0).T
        q_aug = [jnp.where(lane_head == hh, q[t], bias_lanes).astype(BF16) for hh in range(2)]

        ngroups = (tile + group) // group

        def scores(gi, mrun, tile=tile, q_aug=q_aug):
            mrun = list(mrun)
            for u in range(group):
                j = gi * group + u
                rows = pl.ds(pl.multiple_of(j * blk, blk), blk)
                for hh in range(2):
                    s = _dot(q_aug[hh], kaug[hh, rows, :], NT) + bias_ref[hh, jnp.clip(tile - j, 0, 2)]
                    s = s * LOG2E
                    s_scr[hh, j] = s
                    mrun[hh] = jnp.maximum(mrun[hh], jnp.maximum(s[:, :LANE], s[:, LANE:]))
            return tuple(mrun)

        mrun = lax.fori_loop(0, ngroups, scores, (jnp.full((blk, LANE), M_INIT, F32),) * 2)
        m = [jnp.max(mr, axis=-1, keepdims=True) for mr in mrun]

        def weighted(gi, acc, m=m):
            acc = list(acc)
            for u in range(group):
                j = gi * group + u
                rows = pl.ds(pl.multiple_of(j * blk, blk), blk)
                for hh in range(2):
                    acc[hh] = acc[hh] + _dot(jnp.exp2(s_scr[hh, j] - m[hh]), vaug[hh, rows, :])
            return tuple(acc)

        acc = lax.fori_loop(0, ngroups, weighted, (jnp.zeros((blk, LANE), F32),) * 2)
        outs = [a / pltpu.roll(a, DH_A, axis=1) for a in acc]
        gate = g_ref[0, t * blk:(t + 1) * blk, :]
        o_ref[0, t * blk:(t + 1) * blk, :] = _silu(gate) * jnp.where(lane < DH_A, outs[0], outs[1])


def moba_prompt(z, bias):
    bsz, s, _ = z.shape
    nh = bias.shape[0]
    npair = nh // 2
    nb = s // MOBA_BLOCK
    assert s % MOBA_BLOCK == 0 and nb <= DH_A and 2 * DH_A == LANE
    blk = MOBA_BLOCK
    group = next(g for g in (8, 4, 2, 1) if nb % g == 0)
    tps = next(g for g in (4, 2, 1) if nb % g == 0)
    return pl.pallas_call(
        functools.partial(_moba_prompt_kernel, nb=nb, group=group, tiles_per_step=tps),
        grid=(bsz, npair, nb // tps),
        in_specs=[pl.BlockSpec((1, tps * blk, LANE), lambda b, p, i: (b, i, p)),
                  pl.BlockSpec((1, s, LANE), lambda b, p, i: (b, 0, npair + p)),
                  pl.BlockSpec((1, s, LANE), lambda b, p, i: (b, 0, 2 * npair + p)),
                  pl.BlockSpec((1, tps * blk, LANE), lambda b, p, i: (b, i, 3 * npair + p)),
                  pl.BlockSpec((2, 3, blk, blk), lambda b, p, i: (p, 0, 0, 0))],
        out_specs=pl.BlockSpec((1, tps * blk, LANE), lambda b, p, i: (b, i, p)),
        out_shape=jax.ShapeDtypeStruct((bsz, s, nh * DH_A), F32),
        scratch_shapes=[pltpu.VMEM((2, s, LANE), BF16), pltpu.VMEM((2, s, LANE), BF16),
                        pltpu.VMEM((2, LANE, LANE), F32), pltpu.VMEM((2, nb, blk, blk), F32)],
        compiler_params=_params("arbitrary", "arbitrary", "arbitrary"),
        name="moba_prompt",
    )(z, z, z, z, bias)


def _moba_sample_kernel(pt_ref, q_ref, kn_ref, vn_ref, g_ref, *refs, nb, nh, bps):
    del pt_ref
    k_refs, v_refs = refs[:2 * bps], refs[2 * bps:4 * bps]
    bias_past_ref, bias_own_ref, o_ref, qexp, acc_scr, m_all, l_all, gate_all = refs[4 * bps:]
    step = pl.program_id(1)
    t8 = SUBLANE
    rows = nh * t8
    width = nh * DH_A
    page = k_refs[0].shape[2]
    row_head = lax.broadcasted_iota(jnp.int32, (rows, width), 0) // t8
    lane_head = lax.broadcasted_iota(jnp.int32, (rows, width), 1) // DH_A
    head_mask = row_head == lane_head
    lane = lax.broadcasted_iota(jnp.int32, (rows, LANE), 1)

    @pl.when(step == 0)
    def _():
        q8 = q_ref[0] * (DH_A ** -0.5)
        qexp[...] = jnp.where(head_mask, jnp.concatenate([q8] * nh, axis=0), 0.0).astype(BF16)
        m_all[...] = jnp.zeros((rows, LANE), F32)
        l_all[...] = jnp.zeros((rows, LANE), F32)
        gate_all[...] = jnp.zeros((rows, LANE), F32)

    qe = qexp[...]
    m_new, l_new, gate_new = m_all[...], l_all[...], gate_all[...]
    blocks = [step * bps + u for u in range(bps)]
    raw = [(_dot(qe, k_refs[2 * u][0]), _dot(qe, k_refs[2 * u + 1][0])) for u in range(bps)]
    gates = [jnp.sum(ra + rb, axis=-1, keepdims=True) * (1.0 / MOBA_BLOCK) for ra, rb in raw]
    bias = [bias_past_ref[jnp.where(j == nb - 1, 0, 1)] for j in blocks]
    s = [(ra + b[:, :page], rb + b[:, page:]) for (ra, rb), b in zip(raw, bias)]
    m_blk = [jnp.max(jnp.maximum(sa, sb), axis=-1, keepdims=True) for sa, sb in s]
    p = [(jnp.exp(sa - m), jnp.exp(sb - m)) for (sa, sb), m in zip(s, m_blk)]
    l_blk = [jnp.sum(pa + pb, axis=-1, keepdims=True) for pa, pb in p]
    pv = [_dot(pa, v_refs[2 * u][0], NT) + _dot(pb, v_refs[2 * u + 1][0], NT) for u, (pa, pb) in enumerate(p)]
    for u, j in enumerate(blocks):
        acc_scr[j] = pv[u]
        m_new = jnp.where(lane == j, m_blk[u], m_new)
        l_new = jnp.where(lane == j, l_blk[u], l_new)
        gate_new = jnp.where(lane == j, gates[u], gate_new)
    m_all[...] = m_new
    l_all[...] = l_new
    gate_all[...] = gate_new

    @pl.when(step == pl.num_programs(1) - 1)
    def _():
        pad = jnp.zeros((LANE - t8, width), F32)
        s_o = _dot(qe, jnp.concatenate([kn_ref[0], pad], axis=0), NT) + bias_own_ref[...]
        m_o = jnp.max(s_o, axis=-1, keepdims=True)
        p_o = jnp.exp(s_o - m_o)
        l_o = jnp.sum(p_o, axis=-1, keepdims=True)
        acc_o = _dot(p_o, jnp.concatenate([vn_ref[0], pad], axis=0))
        sel = _select_topk(gate_all[...], lane < nb, lane, min(MOBA_TOPK, nb)) > 0
        mm = m_all[...]
        m_tot = jnp.maximum(m_o, jnp.max(jnp.where(sel, mm, M_INIT), axis=-1, keepdims=True))
        w_all = jnp.where(sel, jnp.exp(jnp.where(sel, mm, m_tot) - m_tot), 0.0)
        w_o = jnp.exp(m_o - m_tot)
        l_tot = w_o * l_o + jnp.sum(w_all * l_all[...], axis=-1, keepdims=True)
        tot = w_o * acc_o
        for jb in range(nb):
            tot = tot + w_all[:, jb:jb + 1] * acc_scr[jb]
        tot = jnp.where(head_mask, tot / l_tot, 0.0)
        out8 = tot[0:t8]
        for h in range(1, nh):
            out8 = out8 + tot[h * t8:(h + 1) * t8]
        o_ref[0] = _silu(g_ref[0]) * out8


def moba_sample(z, cache_kt, cache_vt, page0, page_table, bias_past, bias_own):
    bs, t8, _ = z.shape
    width, page = cache_kt.shape[1:]
    nh = width // DH_A
    n_pages = page_table.shape[1]
    assert t8 == SUBLANE and MOBA_BLOCK == 2 * page and (n_pages * page) % MOBA_BLOCK == 0
    nb = n_pages * page // MOBA_BLOCK
    assert nb <= LANE
    rows = nh * t8
    bps = next(g for g in (8, 4, 2, 1) if nb % g == 0)
    zspec = lambda c: pl.BlockSpec((1, t8, width), lambda b, j, pt: (b, 0, c))
    pspec = lambda o: pl.BlockSpec((1, width, page), lambda b, j, pt: (page0 + pt[b, 2 * bps * j + o], 0, 0))
    pages = [pspec(o) for o in range(2 * bps)]
    grid_spec = pltpu.PrefetchScalarGridSpec(
        num_scalar_prefetch=1,
        grid=(bs, nb // bps),
        in_specs=[zspec(0), zspec(1), zspec(2), zspec(3)] + pages + pages
                 + [pl.BlockSpec((2, rows, MOBA_BLOCK), lambda b, j, pt: (0, 0, 0)),
                    pl.BlockSpec((rows, LANE), lambda b, j, pt: (0, 0))],
        out_specs=pl.BlockSpec((1, t8, width), lambda b, j, pt: (b, 0, 0)),
        scratch_shapes=[pltpu.VMEM((rows, width), BF16), pltpu.VMEM((nb, rows, width), F32),
                        pltpu.VMEM((rows, LANE), F32), pltpu.VMEM((rows, LANE), F32),
                        pltpu.VMEM((rows, LANE), F32)])
    return pl.pallas_call(
        functools.partial(_moba_sample_kernel, nb=nb, nh=nh, bps=bps),
        grid_spec=grid_spec,
        out_shape=jax.ShapeDtypeStruct((bs, t8, width), F32),
        compiler_params=_params("arbitrary", "arbitrary"),
        name="moba_sample",
    )(page_table, z, z, z, z, *([cache_kt] * (2 * bps)), *([cache_vt] * (2 * bps)), bias_past, bias_own)


def _head_sums(x, ones_ref):
    hi, lo = _split(x)
    ones = ones_ref[...]
    return jnp.dot(hi, ones, preferred_element_type=F32) + jnp.dot(lo, ones, preferred_element_type=F32)


def _gdn_pre_kernel(xq_ref, xk_ref, xv_ref, pq_ref, pk_ref, pv_ref, w_ref, ones_ref, q_ref, k_ref, v_ref, buf):
    r = xq_ref.shape[1]
    width = xq_ref.shape[2]
    first = pl.program_id(1) == 0
    for idx, (x_ref, p_ref, o_ref) in enumerate(((xq_ref, pq_ref, q_ref), (xk_ref, pk_ref, k_ref),
                                                 (xv_ref, pv_ref, v_ref))):
        @pl.when(first)
        def _(idx=idx, p_ref=p_ref):
            buf[idx, 0:SUBLANE, :] = p_ref[0]

        x = x_ref[0]
        buf[idx, SUBLANE:SUBLANE + r, :] = x
        wcol = slice(idx * width, (idx + 1) * width)
        conv = x * w_ref[CONV_C - 1:CONV_C, wcol]
        for jtap in range(CONV_C - 1):
            off = SUBLANE - (CONV_C - 1) + jtap
            conv = conv + buf[idx, off:off + r, :] * w_ref[jtap:jtap + 1, wcol]
        buf[idx, 0:SUBLANE, :] = x[r - SUBLANE:, :]
        act = _silu(conv)
        if idx < 2:
            act = act * lax.rsqrt(_head_sums(act * act, ones_ref) + RMS_EPS)
        o_ref[0] = act


def _head_ones(width, head):
    ids = np.arange(width) // head
    return jnp.asarray(ids[:, None] == ids[None, :], BF16)


def gdn_pre(z, prev8, w8, tr, col0):
    bsz, t, _ = z.shape
    width = prev8.shape[2] // 3
    zspec = lambda c: pl.BlockSpec((1, tr, width), lambda b, i: (b, i, c))
    pspec = lambda c: pl.BlockSpec((1, SUBLANE, width), lambda b, i: (b, 0, c))
    ospec = pl.BlockSpec((1, tr, width), lambda b, i: (b, i, 0))
    oshape = jax.ShapeDtypeStruct((bsz, t, width), F32)
    return pl.pallas_call(
        _gdn_pre_kernel,
        grid=(bsz, t // tr),
        in_specs=[zspec(col0), zspec(col0 + 1), zspec(col0 + 2), pspec(0), pspec(1), pspec(2),
                  pl.BlockSpec((SUBLANE, 3 * width), lambda b, i: (0, 0)),
                  pl.BlockSpec((width, width), lambda b, i: (0, 0))],
        out_specs=[ospec, ospec, ospec],
        out_shape=[oshape, oshape, oshape],
        scratch_shapes=[pltpu.VMEM((3, tr + SUBLANE, width), F32)],
        compiler_params=_params("arbitrary", "arbitrary"),
        name="gdn_pre",
    )(z, z, z, prev8, prev8, prev8, w8, _head_ones(width, DK_C))


def _gdn_chunk_kernel(q_ref, k_ref, v_ref, ab_ref, alog_ref, dtb_ref, s0_ref, o_ref, sout_ref, s_scr,
                      *, nh, t_valid):
    c = GDN_CHUNK
    ti = pl.program_id(1)
    nseq, tc = q_ref.shape[:2]

    @pl.when(ti == 0)
    def _():
        s_scr[...] = s0_ref[...]

    row = lax.broadcasted_iota(jnp.int32, (c, c), 0)
    col = lax.broadcasted_iota(jnp.int32, (c, c), 1)
    tri = row >= col
    strict = row > col
    eye = jnp.where(row == col, 1.0, 0.0)
    tril_ones = jnp.where(tri, 1.0, 0.0)
    neg_a = -jnp.exp(alog_ref[...])
    dtb = dtb_ref[...]

    def chunk(ci, carry):
        r0 = pl.multiple_of(ci * c, c)
        rows = pl.ds(r0, c)
        gc_all, gc_t, beta_all = [], [], []
        for b in range(nseq):
            ab = ab_ref[b, rows, :]
            g_b = neg_a * (jnp.maximum(ab + dtb, 0.0) + jnp.log(1.0 + jnp.exp(-jnp.abs(ab + dtb))))
            beta_b = 1.0 / (1.0 + jnp.exp(-ab))
            if t_valid is not None:
                live = (ti * tc + r0 + lax.broadcasted_iota(jnp.int32, (c, LANE), 0)) < t_valid
                g_b = jnp.where(live, g_b, 0.0)
                beta_b = jnp.where(live, beta_b, 0.0)
            gc_b = _dot(tril_ones, g_b, NN, exact=True)
            gc_all.append(gc_b)
            gc_t.append(gc_b.T)
            beta_all.append(beta_b)
        units = [(b, h) for b in range(nseq) for h in range(nh)]
        heads = range(len(units))
        gcol = [gc_all[b][:, h:h + 1] for b, h in units]
        bcol = [beta_all[b][:, nh + h:nh + h + 1] for b, h in units]
        decay = [jnp.where(tri, jnp.exp(jnp.where(tri, gcol[u] - gc_t[b][h:h + 1, :], 0.0)), 0.0)
                 for u, (b, h) in enumerate(units)]
        head_cols = lambda x, h: x[:, h * DK_C:(h + 1) * DK_C]
        q_all = [q_ref[b, rows, :] * (DK_C ** -0.5) for b in range(nseq)]
        k_all = [k_ref[b, rows, :] for b in range(nseq)]
        v_all = [v_ref[b, rows, :] for b in range(nseq)]
        q = [head_cols(q_all[b], h) for b, h in units]
        k = [head_cols(k_all[b], h) for b, h in units]
        v = [head_cols(v_all[b], h) for b, h in units]
        kbeta = [k[h] * bcol[h] for h in heads]
        a_mat = [jnp.where(strict, _dot(kbeta[h], k[h], NT) * decay[h], 0.0) for h in heads]
        tinv = [eye - a for a in a_mat]
        power = [_split(a) for a in a_mat]
        for _ in range(int(math.log2(c)) - 1):
            power = [_split(_dot_split(p, p)) for p in power]
            tinv = [t + _dot_split(_split(t), p) for t, p in zip(tinv, power)]
        egc = [jnp.exp(g) for g in gcol]
        u = [_dot(tinv[h], v[h] * bcol[h]) for h in heads]
        w = [_dot(tinv[h], kbeta[h] * egc[h]) for h in heads]
        s_old = [s_scr[b, h] for b, h in units]
        v_new = [u[h] - _dot(w[h], s_old[h]) for h in heads]
        attn = [_dot(q[h], k[h], NT) * decay[h] for h in heads]
        o = [_dot(q[h] * egc[h], s_old[h]) + _dot(attn[h], v_new[h]) for h in heads]
        for b in range(nseq):
            o_ref[b, rows, :] = jnp.concatenate(o[b * nh:(b + 1) * nh], axis=-1)
        for i, (b, h) in enumerate(units):
            g_last = gc_all[b][c - 1:c, h:h + 1]
            s_scr[b, h] = s_old[i] * jnp.exp(g_last) + _dot(k[i] * jnp.exp(g_last - gcol[i]), v_new[i], TN)
        return carry

    lax.fori_loop(0, tc // c, chunk, 0)

    @pl.when(ti == pl.num_programs(1) - 1)
    def _():
        sout_ref[...] = s_scr[...]


def gdn_chunk(q, k, v, ab, ab_col, alog_lane, dtb_lane, s0, tc, t_valid):
    bsz, t, width = q.shape
    nh = width // DK_C
    assert t % tc == 0 and tc % GDN_CHUNK == 0 and DK_C == DV_C
    nseq = 2 if bsz % 2 == 0 else 1
    qspec = pl.BlockSpec((nseq, tc, width), lambda b, i: (b, i, 0))
    sspec = pl.BlockSpec((nseq, nh, DK_C, DV_C), lambda b, i: (b, 0, 0, 0))
    vec = pl.BlockSpec((1, LANE), lambda b, i: (0, 0))
    return pl.pallas_call(
        functools.partial(_gdn_chunk_kernel, nh=nh, t_valid=None if t_valid == t else t_valid),
        grid=(bsz // nseq, t // tc),
        in_specs=[qspec, qspec, qspec, pl.BlockSpec((nseq, tc, LANE), lambda b, i: (b, i, ab_col)), vec, vec, sspec],
        out_specs=[qspec, sspec],
        out_shape=[jax.ShapeDtypeStruct((bsz, t, width), F32),
                   jax.ShapeDtypeStruct((bsz, nh, DK_C, DV_C), F32)],
        scratch_shapes=[pltpu.VMEM((nseq, nh, DK_C, DV_C), F32)],
        compiler_params=_params("arbitrary", "arbitrary"),
        name="gdn_chunk",
    )(q, k, v, ab, alog_lane, dtb_lane, s0)


def _layer_norm(hid, g_ref, b_ref):
    mu = jnp.mean(hid, axis=-1, keepdims=True)
    cen = hid - mu
    var = jnp.mean(cen * cen, axis=-1, keepdims=True)
    return cen * lax.rsqrt(var + LN_EPS) * g_ref[...] + b_ref[...]


def _layer_tail_kernel(ya_ref, bb_ref, cb_ref, hb_ref, gb_ref, pb_ref, cw_ref, oc_ref, gc_ref, nw_ref, ones_ref,
                       wa_ref, wb_ref, wc_ref, x_ref, g1_ref, b1_ref, wq_ref, mk_ref, mv_ref, wo_ref, g2_ref, b2_ref,
                       o_ref, tail_ref, buf, *, alpha):
    nseq, tr, d = x_ref.shape
    flat = lambda ref: ref[...].reshape(nseq * tr, ref.shape[2])

    @pl.when(pl.program_id(1) == 0)
    def _():
        buf[:, 0:SUBLANE, :] = pb_ref[...]

    conv = []
    for b in range(nseq):
        u = cb_ref[b] * hb_ref[b]
        buf[b, SUBLANE:SUBLANE + tr, :] = u
        acc = u * cw_ref[CONV_B - 1:CONV_B, :]
        for jtap in range(CONV_B - 1):
            off = SUBLANE - (CONV_B - 1) + jtap
            acc = acc + buf[b, off:off + tr, :] * cw_ref[jtap:jtap + 1, :]
        tail_ref[b] = u[tr - SUBLANE:, :]
        buf[b, 0:SUBLANE, :] = u[tr - SUBLANE:, :]
        conv.append(acc)
    conv = conv[0] if nseq == 1 else jnp.concatenate(conv, axis=0)
    yb = _silu(flat(gb_ref)) * (flat(bb_ref) * conv)
    oc = flat(oc_ref)
    yc = _silu(flat(gc_ref)) * (oc * lax.rsqrt(_head_sums(oc * oc, ones_ref) * (1.0 / DV_C) + RMS_EPS) * nw_ref[...])
    y = _dot(flat(ya_ref), wa_ref[...]) + _dot(yb, wb_ref[...]) + _dot(yc, wc_ref[...])
    x1 = _layer_norm(alpha * flat(x_ref) + y, g1_ref, b1_ref)
    q = _dot(x1, wq_ref[...]) * ((d // H_X) ** -0.5)
    dh = d // H_X
    units = [(b, slice(h * dh, (h + 1) * dh)) for b in range(nseq) for h in range(H_X)]
    s = [_dot(q[b * tr:(b + 1) * tr, cols], mk_ref[b, :, cols], NT) for b, cols in units]
    p = [jnp.exp(x - jnp.max(x, axis=-1, keepdims=True)) for x in s]
    p = [x / jnp.sum(x, axis=-1, keepdims=True) for x in p]
    pv = [_dot(x, mv_ref[b, :, cols]) for x, (b, cols) in zip(p, units)]
    ctx = [jnp.concatenate(pv[b * H_X:(b + 1) * H_X], axis=-1) for b in range(nseq)]
    ctx = ctx[0] if nseq == 1 else jnp.concatenate(ctx, axis=0)
    out = _layer_norm(alpha * x1 + _dot(ctx, wo_ref[...]), g2_ref, b2_ref)
    o_ref[...] = out.reshape(nseq, tr, d)


def layer_tail(ya, z, convb_col, convb_prev8, convb_w8, oc, gate_col, norm_w, w_out, x, ln1, wq, mk, mv, row0, wo, ln2,
               alpha, tr):
    bsz, t, d = x.shape
    nm = mk.shape[1]
    widths = (ya.shape[2], convb_prev8.shape[2], oc.shape[2])
    offs = (0, widths[0], widths[0] + widths[1])
    nseq = next(n for n in (8, 4, 2, 1) if bsz % n == 0 and row0 % n == 0 and n * tr <= 256)
    row = lambda w: pl.BlockSpec((nseq, tr, w), lambda b, i: (b, i, 0))
    zcol = lambda w, cb: pl.BlockSpec((nseq, tr, w), lambda b, i: (b, i, cb))
    state = pl.BlockSpec((nseq, SUBLANE, widths[1]), lambda b, i: (b, 0, 0))
    full = lambda a: pl.BlockSpec(a.shape, lambda b, i: (0,) * a.ndim)
    mspec = pl.BlockSpec((nseq, nm, d), lambda b, i: (row0 // nseq + b, 0, 0))
    w_parts = [w_out[o:o + w] for o, w in zip(offs, widths)]
    vecs = [v.reshape(1, d) for v in (*ln1, *ln2)]
    nw = jnp.tile(norm_w, widths[2] // DV_C).reshape(1, widths[2])
    ones = _head_ones(widths[2], DV_C)
    return pl.pallas_call(
        functools.partial(_layer_tail_kernel, alpha=alpha),
        grid=(bsz // nseq, t // tr),
        in_specs=[row(widths[0])] + [zcol(widths[1], convb_col + j) for j in range(4)]
                 + [state, full(convb_w8), row(widths[2]), zcol(widths[2], gate_col), full(nw), full(ones)]
                 + [full(w) for w in w_parts]
                 + [row(d), full(vecs[0]), full(vecs[1]), full(wq), mspec, mspec, full(wo), full(vecs[2]), full(vecs[3])],
        out_specs=[row(d), state],
        out_shape=[jax.ShapeDtypeStruct((bsz, t, d), F32), jax.ShapeDtypeStruct((bsz, SUBLANE, widths[1]), F32)],
        scratch_shapes=[pltpu.VMEM((nseq, tr + SUBLANE, widths[1]), F32)],
        compiler_params=_params("arbitrary", "arbitrary"),
        name="layer_tail",
    )(ya, z, z, z, z, convb_prev8, convb_w8, oc, z, nw, ones, *w_parts, x, vecs[0], vecs[1], wq, mk, mv, wo,
      vecs[2], vecs[3])


def _pad_rows(a, rows, axis=1, front=False):
    pad = [(0, 0)] * a.ndim
    pad[axis] = (rows - a.shape[axis], 0) if front else (0, rows - a.shape[axis])
    return jnp.pad(a, pad)


def _tile(t, pref):
    return pref if t % pref == 0 else t


def _layer(x, t_valid, attn_fn, conv_b_prev, conv_c_prev, s0, mk, mv, mem_row0, lw):
    bsz, t, d = x.shape
    m = bsz * t
    w_b = lw["conv_b_w"].shape[1]
    w_c = lw["conv_c_w"].shape[1] // 3
    nh_c = w_c // DV_C
    w_a = lw["w_a"]
    assert w_a == w_b == w_c and w_a % LANE == 0, "column blocks of z are addressed in units of one group width"
    tm = _tile(m, 512)
    tr = _tile(t, 256)

    z = matmul(x.reshape(m, d), lw["w_in"], tm, lw["w_in"].shape[1] // lw["n_split"]).reshape(bsz, t, -1)
    ya = attn_fn(z)
    qc, kc, vc = gdn_pre(z, _pad_rows(conv_c_prev, SUBLANE, front=True), _pad_rows(lw["conv_c_w"], SUBLANE, axis=0),
                         tr, 8)
    ab_col = (12 * w_a) // LANE
    if t % GDN_CHUNK == 0:
        o, s_new = gdn_chunk(qc, kc, vc, z, ab_col, lw["alog_lane"], lw["dtb_lane"], s0, _tile(t, 512), t_valid)
    else:
        tp = GDN_CHUNK
        padt = lambda a: _pad_rows(a, tp, axis=1)
        ab = padt(z[:, :, 12 * w_a:12 * w_a + LANE])
        o, s_new = gdn_chunk(padt(qc), padt(kc), padt(vc), ab, 0, lw["alog_lane"], lw["dtb_lane"], s0, tp, t_valid)
        o = o[:, :t]

    x3, tail_b = layer_tail(ya, z, 4, _pad_rows(conv_b_prev, SUBLANE, front=True),
                            _pad_rows(lw["conv_b_w"], SUBLANE, axis=0), o, 11, lw["norm_w"], lw["w_out"], x,
                            (lw["ln1_g"], lw["ln1_b"]), lw["wq_x"], mk, mv, mem_row0, lw["wo_x"],
                            (lw["ln2_g"], lw["ln2_b"]), lw["alpha"], tr)

    nh_a = w_a // DH_A
    k_new = z[:, :t_valid, w_a:2 * w_a].reshape(bsz, t_valid, nh_a, DH_A)
    v_new = z[:, :t_valid, 2 * w_a:3 * w_a].reshape(bsz, t_valid, nh_a, DH_A)
    tail0 = t - SUBLANE
    conv_b_new = tail_b[:, t_valid - tail0 - (CONV_B - 1):t_valid - tail0]
    conv_c_new = z[:, t_valid - (CONV_C - 1):t_valid, 8 * w_a:11 * w_a]
    return x3, k_new, v_new, conv_b_new, conv_c_new, s_new


def kernel(x_prompt, x_sample, mem_prompt, cache_attn_k, cache_attn_v, cache_mem_k, cache_mem_v, state_conv_b, state_conv_c, state_delta, page_table, w_in, conv_b_w, conv_c_w, gdn_a_log, gdn_dt_bias, gdn_norm_w, w_out, ln1_g, ln1_b, rel_bias, wq_x, wk_x, wv_x, wo_x, ln2_g, ln2_b):
    depth, d, n_in = w_in.shape
    bp, seq, _ = x_prompt.shape
    bs, dec_seq, _ = x_sample.shape
    nh_a = rel_bias.shape[1]
    w_a = nh_a * DH_A
    nh_c = gdn_a_log.shape[1]
    n_mem = mem_prompt.shape[1]
    page = cache_attn_k.shape[2]
    past = page_table.shape[1] * page
    alpha = float((2 * depth) ** 0.25)
    assert n_in == 12 * w_a + 2 * nh_c and dec_seq <= SUBLANE and 2 * nh_c <= LANE
    assert seq >= CONV_C - 1 and dec_seq >= CONV_C - 1 and past % MOBA_BLOCK == 0

    nz = 12 * w_a + 2 * LANE
    n_split = 2
    lane_vec = lambda v: _pad_rows(v.reshape(1, -1).astype(F32), LANE, axis=1)

    bias_prompt = bias_table(rel_bias, (0, MOBA_BLOCK, 2 * MOBA_BLOCK), MOBA_BLOCK, MOBA_BLOCK)
    bias_prompt = bias_prompt.reshape(nh_a, 3, MOBA_BLOCK, MOBA_BLOCK)
    nbf = past // MOBA_BLOCK
    bias_past = bias_table(rel_bias, (past - (nbf - 1) * MOBA_BLOCK, past - (nbf - 2) * MOBA_BLOCK), SUBLANE, MOBA_BLOCK)
    bias_past = bias_past.reshape(nh_a, 2, SUBLANE, MOBA_BLOCK).transpose(1, 0, 2, 3).reshape(2, nh_a * SUBLANE, MOBA_BLOCK)
    bias_own = bias_table(rel_bias, (0,), SUBLANE, LANE).reshape(nh_a * SUBLANE, LANE)

    n_phys = cache_attn_k.shape[1]
    cache_kt = cache_attn_k.transpose(0, 1, 3, 4, 2).reshape(depth * n_phys, w_a, page)
    cache_vt = cache_attn_v.transpose(0, 1, 3, 4, 2).reshape(depth * n_phys, w_a, page)

    mem_k_all = cache_mem_k.reshape(depth * bs, n_mem, d)
    mem_v_all = cache_mem_v.reshape(depth * bs, n_mem, d)

    xp = x_prompt
    xs = _pad_rows(x_sample, SUBLANE)
    outs = [[] for _ in range(12)]
    for l in range(depth):
        lw = dict(
            w_in=_pad_rows(w_in[l], nz, axis=1).astype(BF16), n_split=n_split, w_a=w_a,
            conv_b_w=conv_b_w[l], conv_c_w=conv_c_w[l],
            alog_lane=lane_vec(gdn_a_log[l]), dtb_lane=lane_vec(gdn_dt_bias[l]), norm_w=gdn_norm_w[l],
            w_out=w_out[l].astype(BF16), ln1_g=ln1_g[l], ln1_b=ln1_b[l],
            wq_x=wq_x[l].astype(BF16), wo_x=wo_x[l].astype(BF16), ln2_g=ln2_g[l], ln2_b=ln2_b[l],
            alpha=alpha)
        mem2 = mem_prompt.reshape(bp * n_mem, d)
        tmem = _tile(bp * n_mem, 512)
        mk = matmul(mem2, wk_x[l].astype(BF16), tmem, d).reshape(bp, n_mem, d)
        mv = matmul(mem2, wv_x[l].astype(BF16), tmem, d).reshape(bp, n_mem, d)

        xp, k, v, cb, cc, sd = _layer(
            xp, seq, functools.partial(moba_prompt, bias=bias_prompt),
            jnp.zeros((bp, CONV_B - 1, conv_b_w.shape[2]), F32), jnp.zeros((bp, CONV_C - 1, conv_c_w.shape[2]), F32),
            jnp.zeros((bp, nh_c, DK_C, DV_C), F32), mk, mv, 0, lw)
        for lst, val in zip(outs[:2] + outs[4:6] + [outs[6], outs[8], outs[10]],
                            (k, v, mk.reshape(bp, n_mem, H_X, d // H_X), mv.reshape(bp, n_mem, H_X, d // H_X),
                             cb, cc, sd)):
            lst.append(val)

        attn_s = functools.partial(
            moba_sample, cache_kt=cache_kt, cache_vt=cache_vt, page0=l * n_phys,
            page_table=page_table, bias_past=bias_past, bias_own=bias_own)
        xs, k, v, cb, cc, sd = _layer(
            xs, dec_seq, attn_s, state_conv_b[l], state_conv_c[l], state_delta[l],
            mem_k_all, mem_v_all, l * bs, lw)
        for lst, val in zip(outs[2:4] + [outs[7], outs[9], outs[11]], (k, v, cb, cc, sd)):
            lst.append(val)

    kp, vp, ksm, vsm, mkp, mvp, cbp, cbs, ccp, ccs, sdp, sds = (jnp.stack(o) for o in outs)
    return (xp, xs[:, :dec_seq], kp, vp, ksm, vsm, mkp, mvp, cbp, cbs, ccp, ccs, sdp, sds)
```
